```python
import jax, jax.numpy as jnp
from jax import lax
import numpy as np

D_MODEL = 1024
BATCH = 32
SEQ = 256
DEPTH = 1
DEC_BATCH = 4
DEC_SEQ = 2048
PAST_LEN = 256

GRID_W = 64
GLA_WIDTH = 512
GLA_HEADS = 4
GLA_DK = 64
GLA_DV = 128
QK_W = GLA_HEADS * GLA_DK
GLA_LOWRANK = 16
GLA_TAU = 16.0
GLA_CHUNK = 64
CONV_WIDTH = 512
CONV_K = 3
MIX_WIDTH = GLA_WIDTH + CONV_WIDTH
D_FF = 2816
N_MOD = 9
EPS = 1e-6
SPLITS = [QK_W, QK_W, GLA_WIDTH, GLA_WIDTH, GLA_LOWRANK, GLA_LOWRANK, CONV_WIDTH, CONV_WIDTH, CONV_WIDTH]
IN_COLS = sum(SPLITS)
SPLIT_IDX = list(np.cumsum(SPLITS)[:-1])

kernel_name = "hybrid_gla_shortconv_diffusion_step"


def rmsnorm(x, g):
    xf = x.astype(jnp.float32)
    y = xf * lax.rsqrt(jnp.mean(xf * xf, axis=-1, keepdims=True) + EPS)
    return (y * g.astype(jnp.float32)).astype(x.dtype)


def ada_mod(cvec, w, b):
    return (jax.nn.silu(cvec) @ w + b).reshape(cvec.shape[0], N_MOD, D_MODEL)


def swiglu(h, w1, w3, w2):
    return (jax.nn.silu(h @ w1) * (h @ w3)) @ w2


def conv3_centred(u, w):
    up = jnp.pad(u, [(0, 0)] * (u.ndim - 2) + [(1, 1), (0, 0)])
    return w[0] * up[..., :-2, :] + w[1] * up[..., 1:-1, :] + w[2] * up[..., 2:, :]


def gla_chunked(q, k, v, log_a, s0):
    b, h, l, _ = q.shape
    dv = v.shape[-1]
    n = l // GLA_CHUNK

    def to_chunks(t):
        return jnp.moveaxis(t.reshape(b, h, n, GLA_CHUNK, t.shape[-1]), 2, 0)

    mask = jnp.tril(jnp.ones((GLA_CHUNK, GLA_CHUNK), bool))[:, :, None]

    def step(s, inp):
        qc, kc, vc, ac = inp
        cum = jnp.cumsum(ac, axis=2)
        o_inter = jnp.einsum('bhtd,bhde->bhte', qc * jnp.exp(cum), s)
        diff = cum[:, :, :, None, :] - cum[:, :, None, :, :]
        decay = jnp.where(mask, jnp.exp(jnp.where(mask, diff, 0.0)), 0.0)
        scores = jnp.einsum('bhtd,bhsd,bhtsd->bhts', qc, kc, decay)
        o_intra = jnp.einsum('bhts,bhse->bhte', scores, vc)
        total = cum[:, :, -1:, :]
        s_new = jnp.exp(total[:, :, 0, :])[..., None] * s + jnp.einsum(
            'bhsd,bhse->bhde', kc * jnp.exp(total - cum), vc)
        return s_new, o_inter + o_intra

    s_fin, o = lax.scan(step, s0, (to_chunks(q), to_chunks(k), to_chunks(v), to_chunks(log_a)))
    return jnp.moveaxis(o, 0, 2).reshape(b, h, l, dv), s_fin


def heads(t, dh):
    b, l, _ = t.shape
    return jnp.transpose(t.reshape(b, l, GLA_HEADS, dh), (0, 2, 1, 3)).astype(jnp.float32)


def mixer(h, w_in, w_decay, b_decay, gla_norm, conv_w, w_out, s0_f, s0_b, rows):
    b, l, _ = h.shape
    proj = h @ w_in
    q, k, v, g, lr_f, lr_b, cb, cc, ch = jnp.split(proj, SPLIT_IDX, axis=-1)
    qh = heads(q, GLA_DK) * (GLA_DK ** -0.5)
    kh = heads(k, GLA_DK)
    vh = heads(v, GLA_DV)
    la_f = heads(jax.nn.log_sigmoid((lr_f @ w_decay[0] + b_decay[0]).astype(jnp.float32)) / GLA_TAU, GLA_DK)
    la_b = heads(jax.nn.log_sigmoid((lr_b @ w_decay[1] + b_decay[1]).astype(jnp.float32)) / GLA_TAU, GLA_DK)
    o_f, s_f = gla_chunked(qh, kh, vh, la_f, s0_f.astype(jnp.float32))
    fl = lambda t: jnp.flip(t, axis=2)
    o_b, s_b = gla_chunked(fl(qh), fl(kh), fl(vh), fl(la_b), s0_b.astype(jnp.float32))
    o = o_f + fl(o_b)
    o = o * lax.rsqrt(jnp.mean(o * o, axis=-1, keepdims=True) + EPS)
    o = jnp.transpose(o, (0, 2, 1, 3)).reshape(b, l, GLA_WIDTH) * gla_norm.astype(jnp.float32)
    o_gla = (o * jax.nn.silu(g.astype(jnp.float32))).astype(h.dtype)
    u = cc * ch
    if rows is None:
        cu = conv3_centred(u, conv_w)
    else:
        cu = conv3_centred(u.reshape(b, rows, GRID_W, CONV_WIDTH), conv_w).reshape(b, l, CONV_WIDTH)
    o_conv = cb * cu
    return jnp.concatenate([o_gla, o_conv], axis=-1) @ w_out, s_f, s_b


def block(x, mods, s0_f, s0_b, rows, norm_ffn1, w1_ffn1, w3_ffn1, w2_ffn1, norm_mix, w_in,
          w_decay, b_decay, gla_norm, conv_w, w_out, norm_ffn2, w1_ffn2, w3_ffn2, w2_ffn2):
    m = lambda i: mods[:, i, None, :]
    h = rmsnorm(x, norm_ffn1) * (1.0 + m(1)) + m(0)
    x = x + 0.5 * m(2) * swiglu(h, w1_ffn1, w3_ffn1, w2_ffn1)
    h = rmsnorm(x, norm_mix) * (1.0 + m(4)) + m(3)
    mix, s_f, s_b = mixer(h, w_in, w_decay, b_decay, gla_norm, conv_w, w_out, s0_f, s0_b, rows)
    x = x + m(5) * mix
    h = rmsnorm(x, norm_ffn2) * (1.0 + m(7)) + m(6)
    x = x + 0.5 * m(8) * swiglu(h, w1_ffn2, w3_ffn2, w2_ffn2)
    return x, s_f, s_b


def setup_inputs(seed: int = 0) -> dict:
    key = jax.random.key(seed)
    ks = jax.random.split(key, 32)
    nrm = lambda i, shape, s: jax.random.normal(ks[i], shape, jnp.float32) * s
    gain = lambda i, shape: 1.0 + 0.05 * jax.random.normal(ks[i], shape, jnp.float32)
    return {
        "x_prompt": nrm(0, (BATCH, SEQ, D_MODEL), 1.0),
        "x_sample": nrm(1, (DEC_BATCH, DEC_SEQ, D_MODEL), 1.0),
        "state_gla": nrm(2, (DEC_BATCH, DEPTH, 2, GLA_HEADS, GLA_DK, GLA_DV), 0.5),
        "c": nrm(3, (DEC_BATCH, D_MODEL), 1.0),
        "c_ctx": nrm(4, (D_MODEL,), 1.0),
        "w_ada": nrm(5, (DEPTH, D_MODEL, N_MOD * D_MODEL), D_MODEL ** -0.5),
        "b_ada": nrm(6, (DEPTH, N_MOD * D_MODEL), 0.02),
        "norm_ffn1": gain(7, (DEPTH, D_MODEL)),
        "w1_ffn1": nrm(8, (DEPTH, D_MODEL, D_FF), D_MODEL ** -0.5),
        "w3_ffn1": nrm(9, (DEPTH, D_MODEL, D_FF), D_MODEL ** -0.5),
        "w2_ffn1": nrm(10, (DEPTH, D_FF, D_MODEL), D_FF ** -0.5),
        "norm_mix": gain(11, (DEPTH, D_MODEL)),
        "w_in": nrm(12, (DEPTH, D_MODEL, IN_COLS), D_MODEL ** -0.5),
        "w_decay": nrm(13, (DEPTH, 2, GLA_LOWRANK, QK_W), GLA_LOWRANK ** -0.5),
        "b_decay": nrm(14, (DEPTH, 2, QK_W), 0.1),
        "gla_norm": gain(15, (DEPTH, GLA_WIDTH)),
        "conv_w": nrm(16, (DEPTH, CONV_K, CONV_WIDTH), CONV_K ** -0.5),
        "w_out": nrm(17, (DEPTH, MIX_WIDTH, D_MODEL), MIX_WIDTH ** -0.5),
        "norm_ffn2": gain(18, (DEPTH, D_MODEL)),
        "w1_ffn2": nrm(19, (DEPTH, D_MODEL, D_FF), D_MODEL ** -0.5),
        "w3_ffn2": nrm(20, (DEPTH, D_MODEL, D_FF), D_MODEL ** -0.5),
        "w2_ffn2": nrm(21, (DEPTH, D_FF, D_MODEL), D_FF ** -0.5),
        "final_norm": gain(22, (D_MODEL,)),
    }


def reference(x_prompt, x_sample, state_gla, c, c_ctx, w_ada, b_ada, norm_ffn1, w1_ffn1, w3_ffn1,
              w2_ffn1, norm_mix, w_in, w_decay, b_decay, gla_norm, conv_w, w_out, norm_ffn2,
              w1_ffn2, w3_ffn2, w2_ffn2, final_norm):
    rows = x_sample.shape[1] // GRID_W
    xp, xs = x_prompt, x_sample
    new_states = []
    for l in range(DEPTH):
        lw = (norm_ffn1[l], w1_ffn1[l], w3_ffn1[l], w2_ffn1[l], norm_mix[l], w_in[l], w_decay[l],
              b_decay[l], gla_norm[l], conv_w[l], w_out[l], norm_ffn2[l], w1_ffn2[l], w3_ffn2[l], w2_ffn2[l])
        mod_ctx = ada_mod(c_ctx[None, :], w_ada[l], b_ada[l])
        zeros = jnp.zeros((xp.shape[0], GLA_HEADS, GLA_DK, GLA_DV), jnp.float32)
        xp, s_f, s_b = block(xp, mod_ctx, zeros, zeros, None, *lw)
        new_states.append(jnp.stack([s_f, s_b], axis=1))
        mod_lat = ada_mod(c, w_ada[l], b_ada[l])
        xs, _, _ = block(xs, mod_lat, state_gla[:, l, 0], state_gla[:, l, 1], rows, *lw)
    y_prompt = rmsnorm(xp, final_norm)
    y_sample = rmsnorm(xs, final_norm)
    new_state_gla = jnp.stack(new_states, axis=1)
    return (y_prompt, y_sample, new_state_gla)
```

```python
import functools

import numpy as np
import jax
import jax.numpy as jnp
from jax import lax
from jax.experimental import pallas as pl
from jax.experimental.pallas import tpu as pltpu

F32 = jnp.float32
BF16 = jnp.bfloat16

D_MODEL = 1024
D_FF = 2816
N_MOD = 9
EPS = 1e-6
GLA_HEADS = 4
GLA_DK = 64
GLA_DV = 128
QK_W = GLA_HEADS * GLA_DK
GLA_WIDTH = GLA_HEADS * GLA_DV
CONV_WIDTH = 512
GLA_LOWRANK = 16
GLA_TAU = 16.0
GRID_W = 64
SPLITS = [QK_W, QK_W, GLA_WIDTH, GLA_WIDTH, GLA_LOWRANK, GLA_LOWRANK, CONV_WIDTH, CONV_WIDTH, CONV_WIDTH]

LANES = 128
CHUNK = 128
N_LEVELS = 7
TILE_FFN = 256
TILE_MIX = 256
LR_PAD = LANES
PROJ_COLS = 2 * QK_W + 2 * GLA_WIDTH + 3 * CONV_WIDTH + LR_PAD
VMEM_LIMIT_FFN = 56 * 1024 * 1024
VMEM_LIMIT_MIX = 48 * 1024 * 1024

_NT = (((1,), (1,)), ((), ()))


def _dot(a, b):
    return jnp.dot(a, b, preferred_element_type=F32)


def _dot_nt(a, b):
    return lax.dot_general(a, b, _NT, preferred_element_type=F32)


def _silu(x):
    return x / (1.0 + jnp.exp(-x))


def _rms(x):
    return x * lax.rsqrt(jnp.mean(x * x, axis=-1, keepdims=True) + EPS)


def _const_spec(shape):
    nd = len(shape)
    return pl.BlockSpec(shape, lambda *_: (0,) * nd, pipeline_mode=pl.Buffered(1))


def _ada_kernel(c_ref, w_ref, b_ref, o_ref):
    a = _silu(c_ref[...]).astype(BF16)
    o_ref[...] = _dot(a, w_ref[...].astype(BF16)) + b_ref[...]


def _ada_mod(cvec, w, b):
    rows = cvec.shape[0]
    n = w.shape[1]
    tn = D_MODEL
    return pl.pallas_call(
        _ada_kernel,
        grid=(n // tn,),
        in_specs=[
            pl.BlockSpec((rows, D_MODEL), lambda i: (0, 0)),
            pl.BlockSpec((D_MODEL, tn), lambda i: (0, i)),
            pl.BlockSpec((1, tn), lambda i: (0, i)),
        ],
        out_specs=pl.BlockSpec((rows, tn), lambda i: (0, i)),
        out_shape=jax.ShapeDtypeStruct((rows, n), F32),
        name="ada_mod",
    )(cvec, w, b.reshape(1, n))


def _ffn_core(x, mod_ref, base, norm_ref, w1_ref, w3_ref, w2_ref):
    shift = mod_ref[base:base + 1, :]
    scale = mod_ref[base + 1:base + 2, :]
    gate = mod_ref[base + 2:base + 3, :]
    h = (_rms(x) * norm_ref[...] * (1.0 + scale) + shift).astype(BF16)
    a = _dot(h, w1_ref[...])
    b = _dot(h, w3_ref[...])
    act = (_silu(a) * b).astype(BF16)
    return x + (0.5 * gate) * _dot(act, w2_ref[...])


def _ffn_proj_kernel(x_ref, mod_ref, n1_ref, w1_ref, w3_ref, w2_ref, n2_ref, win_ref, wkt_ref,
                     xo_ref, proj_ref, kt_ref):
    y = _ffn_core(x_ref[...], mod_ref, 0, n1_ref, w1_ref, w3_ref, w2_ref)
    xo_ref[...] = y
    h2 = (_rms(y) * n2_ref[...] * (1.0 + mod_ref[4:5, :]) + mod_ref[3:4, :]).astype(BF16)
    proj_ref[...] = _dot(h2, win_ref[...]).astype(BF16)
    kt_ref[...] = _dot_nt(wkt_ref[...], h2).astype(BF16)


def _ffn_final_kernel(x_ref, mod_ref, n1_ref, w1_ref, w3_ref, w2_ref, fn_ref, yo_ref):
    y = _ffn_core(x_ref[...], mod_ref, 6, n1_ref, w1_ref, w3_ref, w2_ref)
    yo_ref[...] = _rms(y) * fn_ref[...]


def _mod_spec(mod_base, rows_per_mod, tile):
    per = rows_per_mod // tile
    return pl.BlockSpec((None, N_MOD, D_MODEL), lambda i: (mod_base + i // per, 0, 0))


def _ffn_proj(x, mods, mod_base, rows_per_mod, n1, w1, w3, w2, n2, win, wkt):
    n = x.shape[0]
    tm = TILE_FFN
    row = lambda i: (i, 0)
    return pl.pallas_call(
        _ffn_proj_kernel,
        grid=(n // tm,),
        in_specs=[
            pl.BlockSpec((tm, D_MODEL), row),
            _mod_spec(mod_base, rows_per_mod, tm),
            _const_spec((1, D_MODEL)),
            _const_spec((D_MODEL, D_FF)),
            _const_spec((D_MODEL, D_FF)),
            _const_spec((D_FF, D_MODEL)),
            _const_spec((1, D_MODEL)),
            _const_spec((D_MODEL, PROJ_COLS)),
            _const_spec((QK_W, D_MODEL)),
        ],
        out_specs=[
            pl.BlockSpec((tm, D_MODEL), row),
            pl.BlockSpec((tm, PROJ_COLS), row),
            pl.BlockSpec((QK_W, tm), lambda i: (0, i)),
        ],
        out_shape=[
            jax.ShapeDtypeStruct((n, D_MODEL), F32),
            jax.ShapeDtypeStruct((n, PROJ_COLS), BF16),
            jax.ShapeDtypeStruct((QK_W, n), BF16),
        ],
        compiler_params=pltpu.CompilerParams(
            dimension_semantics=("arbitrary",), vmem_limit_bytes=VMEM_LIMIT_FFN),
        name="ffn1_proj",
    )(x, mods, n1, w1, w3, w2, n2, win, wkt)


def _ffn_final(x, mods, mod_base, rows_per_mod, n1, w1, w3, w2, fn):
    n = x.shape[0]
    tm = TILE_FFN
    row = lambda i: (i, 0)
    return pl.pallas_call(
        _ffn_final_kernel,
        grid=(n // tm,),
        in_specs=[
            pl.BlockSpec((tm, D_MODEL), row),
            _mod_spec(mod_base, rows_per_mod, tm),
            _const_spec((1, D_MODEL)),
            _const_spec((D_MODEL, D_FF)),
            _const_spec((D_MODEL, D_FF)),
            _const_spec((D_FF, D_MODEL)),
            _const_spec((1, D_MODEL)),
        ],
        out_specs=pl.BlockSpec((tm, D_MODEL), row),
        out_shape=jax.ShapeDtypeStruct((n, D_MODEL), F32),
        compiler_params=pltpu.CompilerParams(
            dimension_semantics=("arbitrary",), vmem_limit_bytes=VMEM_LIMIT_FFN),
        name="ffn2_final",
    )(x, mods, n1, w1, w3, w2, fn)


def _gla_tables():
    c = CHUNK
    t = np.arange(c)[:, None]
    u = np.arange(c)[None, :]
    dsum = np.zeros((2, N_LEVELS + 1, c, c), np.float32)
    mask = np.zeros((2, N_LEVELS + 1, c, c), np.float32)
    drest = np.zeros((2, c, c), np.float32)
    for lvl in range(N_LEVELS):
        m = c >> (lvl + 1)
        mid = (t // (2 * m)) * (2 * m) + m
        mid_u = (u // (2 * m)) * (2 * m) + m
        same = (t // (2 * m)) == (u // (2 * m))
        hi_t, hi_u = t >= mid, u >= mid_u
        dsum[0, lvl] = np.where(hi_t, (u >= mid) & (u <= t), (u > t) & (u < mid))
        mask[0, lvl] = same & hi_t & ~hi_u
        dsum[1, lvl] = np.where(hi_t, (u >= mid) & (u < t), (u >= t) & (u < mid))
        mask[1, lvl] = same & ~hi_t & hi_u
    dsum[0, N_LEVELS] = u <= t
    dsum[1, N_LEVELS] = u >= t
    mask[:, N_LEVELS] = t == u
    drest[0] = u > t
    drest[1] = u < t
    dsum = dsum.reshape(2, (N_LEVELS + 1) * c, c)
    dsum2 = np.concatenate([dsum, dsum], axis=2)
    rest_t = np.transpose(drest, (0, 2, 1))
    rest1 = np.concatenate([rest_t, np.ones_like(rest_t)], axis=2)
    rest2 = np.concatenate([rest1, rest1], axis=1)
    mask2 = np.concatenate([mask, mask], axis=2)
    return (jnp.asarray(dsum2, BF16), jnp.asarray(rest2, BF16), jnp.asarray(mask2, F32))


def _split_hi_lo(x):
    hi = x.astype(BF16)
    lo = (x - hi.astype(F32)).astype(BF16)
    return hi, lo


def _log_sigmoid(x):
    return jnp.minimum(x, 0.0) - jnp.log(1.0 + jnp.exp(-jnp.abs(x)))


def _gla_chunk(q, k, v, kt, la_hl, lat_hl, dsum_ref, rest_ref, mask_ref, s_prev):
    e_all = jnp.exp(_dot(dsum_ref[...], la_hl))
    r = _dot(lat_hl, rest_ref[...])
    k_rest_t = (kt * jnp.exp(r[:, :CHUNK])).astype(BF16)
    gamma = jnp.exp(r[:, CHUNK:])
    lane = lax.broadcasted_iota(jnp.int32, (CHUNK, LANES), 1)
    outs, upd = [], []
    for pr in range(GLA_HEADS // 2):
        ls = slice(LANES * pr, LANES * (pr + 1))
        qp, kp = q[:, ls], k[:, ls]
        qm = jnp.concatenate([jnp.where(lane < GLA_DK, qp, 0.0),
                              jnp.where(lane >= GLA_DK, qp, 0.0)], axis=0)
        scores = _dot_nt(qm.astype(BF16), kp.astype(BF16)) * mask_ref[N_LEVELS]
        for lvl in range(N_LEVELS):
            e = e_all[CHUNK * lvl:CHUNK * (lvl + 1), ls]
            lhs = (qm * jnp.concatenate([e, e], axis=0)).astype(BF16)
            rhs = (kp * e).astype(BF16)
            scores = scores + _dot_nt(lhs, rhs) * mask_ref[lvl]
        e = e_all[CHUNK * N_LEVELS:, ls]
        q_in = (qm * jnp.concatenate([e, e], axis=0)).astype(BF16)
        o_inter = _dot(q_in, s_prev[LANES * pr:LANES * (pr + 1), :].astype(BF16))
        scores = scores.astype(BF16)
        for hh in range(2):
            h = 2 * pr + hh
            vh = v[:, GLA_DV * h:GLA_DV * (h + 1)]
            rows = slice(CHUNK * hh, CHUNK * (hh + 1))
            outs.append(o_inter[rows] + _dot(scores[rows], vh))
            upd.append(_dot(k_rest_t[GLA_DK * h:GLA_DK * (h + 1), :], vh))
    s_new = gamma * s_prev + jnp.concatenate(upd, axis=0)
    return outs, s_new


def _mixer_kernel(*refs, nt, period, has_s0):
    (qk_ref, v_ref, g_ref, cv_ref, lr_ref, kt_ref, x_ref, mod_ref) = refs[:8]
    refs = refs[8:]
    if has_s0:
        s0_ref, refs = refs[0], refs[1:]
    (wd_ref, bd_ref, wdt_ref, bdt_ref, dsum_ref, rest_ref, mask_ref, gn_ref, cw_ref, wout_ref,
     xo_ref, st_ref, o_ref, s_ref) = refs
    p = pl.program_id(1)
    j = pl.program_id(2)
    n_chunks = TILE_MIX // CHUNK

    def direction(d):
        @pl.when(j == 0)
        def _():
            s_ref[...] = s0_ref[d] if has_s0 else jnp.zeros(s_ref.shape, F32)

        tile = j if d == 0 else nt - 1 - j
        lr = lr_ref[...]
        la = _log_sigmoid(_dot(lr, wd_ref[...]) + bd_ref[...]) * (1.0 / GLA_TAU)
        lat = _log_sigmoid(_dot_nt(wdt_ref[...], lr) + bdt_ref[...]) * (1.0 / GLA_TAU)
        la_hi, la_lo = _split_hi_lo(la)
        lat_hi, lat_lo = _split_hi_lo(lat)
        q_all = qk_ref[:, :QK_W].astype(F32) * (GLA_DK ** -0.5)
        k_all = qk_ref[:, QK_W:].astype(F32)
        kt_all = kt_ref[...].astype(F32)
        s = s_ref[...]
        for c in (range(n_chunks) if d == 0 else reversed(range(n_chunks))):
            rows = slice(CHUNK * c, CHUNK * (c + 1))
            la_hl = jnp.concatenate([la_hi[rows], la_lo[rows]], axis=0)
            lat_hl = jnp.concatenate([lat_hi[:, rows], lat_lo[:, rows]], axis=1)
            outs, s = _gla_chunk(q_all[rows], k_all[rows], v_ref[rows, :], kt_all[:, rows],
                                 la_hl, lat_hl, dsum_ref, rest_ref, mask_ref, s)
            base = pl.multiple_of(tile * TILE_MIX + CHUNK * c, CHUNK)
            for h in range(GLA_HEADS):
                cols = slice(GLA_DV * h, GLA_DV * (h + 1))
                if d == 0:
                    o_ref[pl.ds(base, CHUNK), cols] = outs[h]
                else:
                    o_ref[pl.ds(base, CHUNK), cols] += outs[h]
        s_ref[...] = s

        @pl.when(j == nt - 1)
        def _():
            st_ref[d] = s

        if d == 1:
            base = pl.multiple_of(tile * TILE_MIX, TILE_MIX)
            o = o_ref[pl.ds(base, TILE_MIX), :]
            g = g_ref[...].astype(F32)
            parts = []
            for h in range(GLA_HEADS):
                cols = slice(GLA_DV * h, GLA_DV * (h + 1))
                parts.append(_rms(o[:, cols]) * gn_ref[:, cols] * _silu(g[:, cols]))
            cb = cv_ref[:, :CONV_WIDTH].astype(F32)
            u = cv_ref[:, CONV_WIDTH:2 * CONV_WIDTH].astype(F32) * cv_ref[:, 2 * CONV_WIDTH:].astype(F32)
            pos = lax.broadcasted_iota(jnp.int32, u.shape, 0) & (period - 1)
            u_prev = jnp.where(pos == 0, 0.0, pltpu.roll(u, 1, 0))
            u_next = jnp.where(pos == period - 1, 0.0, pltpu.roll(u, TILE_MIX - 1, 0))
            cu = cw_ref[0:1, :] * u_prev + cw_ref[1:2, :] * u + cw_ref[2:3, :] * u_next
            parts.append(cb * cu)
            mixed = jnp.concatenate(parts, axis=-1).astype(BF16)
            xo_ref[...] = x_ref[...] + mod_ref[5:6, :] * _dot(mixed, wout_ref[...])

    @pl.when(p == 0)
    def _():
        direction(0)

    @pl.when(p == 1)
    def _():
        direction(1)


def _mixer(proj, kt, x, mods, mod_base, mod_step, s0, seq_len, period, lw):
    n = x.shape[0]
    batch = n // seq_len
    t = TILE_MIX
    nt = seq_len // t
    assert period & (period - 1) == 0 and t % period == 0

    def tile(p, j):
        return j + p * (nt - 1 - 2 * j)

    def late(p, j):
        return (nt - 1) - p * j

    scan_row = lambda b, p, j: (b * nt + tile(p, j), 0)
    in_specs = [
        pl.BlockSpec((t, 2 * QK_W), scan_row),
        pl.BlockSpec((t, GLA_WIDTH), lambda b, p, j: (b * nt + tile(p, j), 1)),
        pl.BlockSpec((t, GLA_WIDTH), lambda b, p, j: (b * nt + late(p, j), 2)),
        pl.BlockSpec((t, 3 * CONV_WIDTH), lambda b, p, j: (b * nt + late(p, j), 1)),
        pl.BlockSpec((t, LR_PAD), lambda b, p, j: (b * nt + tile(p, j), (PROJ_COLS - LR_PAD) // LR_PAD)),
        pl.BlockSpec((QK_W, t), lambda b, p, j: (0, b * nt + tile(p, j))),
        pl.BlockSpec((t, D_MODEL), lambda b, p, j: (b * nt + late(p, j), 0)),
        pl.BlockSpec((None, N_MOD, D_MODEL), lambda b, p, j: (mod_base + mod_step * b, 0, 0)),
    ]
    args = [proj, proj, proj, proj, proj, kt, x, mods]
    if s0 is not None:
        in_specs.append(pl.BlockSpec((None, 2, QK_W, GLA_DV), lambda b, p, j: (b, 0, 0, 0)))
        args.append(s0)
    by_dir = lambda shape: pl.BlockSpec((None,) + shape, lambda b, p, j: (p,) + (0,) * len(shape))
    dsum, rest, mask = lw["tables"]
    in_specs += [
        by_dir((LR_PAD, QK_W)), by_dir((1, QK_W)), by_dir((QK_W, LR_PAD)), by_dir((QK_W, t)),
        by_dir(dsum.shape[1:]), by_dir(rest.shape[1:]), by_dir(mask.shape[1:]),
        _const_spec((1, GLA_WIDTH)), _const_spec((3, CONV_WIDTH)), _const_spec((D_MODEL, D_MODEL)),
    ]
    args += [lw["wd"], lw["bd"], lw["wdt"], lw["bdt"], dsum, rest, mask,
             lw["gla_norm"], lw["conv_w"], lw["w_out"]]
    return pl.pallas_call(
        functools.partial(_mixer_kernel, nt=nt, period=period, has_s0=s0 is not None),
        grid=(batch, 2, nt),
        in_specs=in_specs,
        out_specs=[
            pl.BlockSpec((t, D_MODEL), lambda b, p, j: (b * nt + late(p, j), 0)),
            pl.BlockSpec((None, 2, QK_W, GLA_DV), lambda b, p, j: (b, 0, 0, 0)),
        ],
        out_shape=[
            jax.ShapeDtypeStruct((n, D_MODEL), F32),
            jax.ShapeDtypeStruct((batch, 2, QK_W, GLA_DV), F32),
        ],
        scratch_shapes=[
            pltpu.VMEM((seq_len, GLA_WIDTH), F32),
            pltpu.VMEM((QK_W, GLA_DV), F32),
        ],
        compiler_params=pltpu.CompilerParams(
            dimension_semantics=("arbitrary", "arbitrary", "arbitrary"),
            vmem_limit_bytes=VMEM_LIMIT_MIX),
        name="mixer",
    )(*args)


def _prep_layer(l, norm_ffn1, w1_ffn1, w3_ffn1, w2_ffn1, norm_mix, w_in, w_decay, b_decay, gla_norm,
                conv_w, w_out, norm_ffn2, w1_ffn2, w3_ffn2, w2_ffn2, tables):
    q, k, v, g, lr_f, lr_b, cb, cc, ch = jnp.split(w_in[l], list(np.cumsum(SPLITS)[:-1]), axis=-1)
    pad = jnp.zeros((D_MODEL, LR_PAD - 2 * GLA_LOWRANK), F32)
    win = jnp.concatenate([q, k, v, g, cb, cc, ch, lr_f, lr_b, pad], axis=-1).astype(BF16)
    wd = jnp.zeros((2, LR_PAD, QK_W), F32)
    wd = wd.at[0, :GLA_LOWRANK].set(w_decay[l, 0]).at[1, GLA_LOWRANK:2 * GLA_LOWRANK].set(w_decay[l, 1])
    bd = b_decay[l].reshape(2, 1, QK_W)
    return dict(
        n1=norm_ffn1[l].reshape(1, -1), w1a=w1_ffn1[l].astype(BF16), w3a=w3_ffn1[l].astype(BF16),
        w2a=w2_ffn1[l].astype(BF16),
        n_mix=norm_mix[l].reshape(1, -1), win=win, wkt=k.T.astype(BF16),
        wd=wd.astype(BF16), bd=bd, wdt=jnp.transpose(wd, (0, 2, 1)).astype(BF16),
        bdt=jnp.broadcast_to(jnp.transpose(bd, (0, 2, 1)), (2, QK_W, TILE_MIX)),
        gla_norm=gla_norm[l].reshape(1, -1), conv_w=conv_w[l], w_out=w_out[l].astype(BF16),
        n2=norm_ffn2[l].reshape(1, -1), w1b=w1_ffn2[l].astype(BF16), w3b=w3_ffn2[l].astype(BF16),
        w2b=w2_ffn2[l].astype(BF16), tables=tables)


def kernel(x_prompt, x_sample, state_gla, c, c_ctx, w_ada, b_ada, norm_ffn1, w1_ffn1, w3_ffn1, w2_ffn1,
           norm_mix, w_in, w_decay, b_decay, gla_norm, conv_w, w_out, norm_ffn2, w1_ffn2, w3_ffn2,
           w2_ffn2, final_norm):
    batch, seq, _ = x_prompt.shape
    dec_batch, dec_seq, _ = x_sample.shape
    depth = w_ada.shape[0]
    xp = x_prompt.reshape(batch * seq, D_MODEL)
    xs = x_sample.reshape(dec_batch * dec_seq, D_MODEL)
    mod_rows = 8
    assert 1 + dec_batch <= mod_rows
    cvec = jnp.concatenate([c_ctx[None, :], c, jnp.zeros((mod_rows - 1 - dec_batch, D_MODEL), F32)], axis=0)
    fn = final_norm.reshape(1, -1)
    tables = _gla_tables()
    states = []
    yp = ys = None
    for l in range(depth):
        lw = _prep_layer(l, norm_ffn1, w1_ffn1, w3_ffn1, w2_ffn1, norm_mix, w_in, w_decay, b_decay,
                         gla_norm, conv_w, w_out, norm_ffn2, w1_ffn2, w3_ffn2, w2_ffn2, tables)
        mods = _ada_mod(cvec, w_ada[l], b_ada[l]).reshape(mod_rows, N_MOD, D_MODEL)
        s0 = state_gla[:, l].reshape(dec_batch, 2, QK_W, GLA_DV)
        last = l == depth - 1
        new = []
        for (x, mod_base, mod_step, rows_per_mod, init, slen, period) in (
                (xp, 0, 0, batch * seq, None, seq, seq),
                (xs, 1, 1, dec_seq, s0, dec_seq, GRID_W)):
            x1, proj, kt = _ffn_proj(x, mods, mod_base, rows_per_mod, lw["n1"], lw["w1a"], lw["w3a"],
                                     lw["w2a"], lw["n_mix"], lw["win"], lw["wkt"])
            x2, st = _mixer(proj, kt, x1, mods, mod_base, mod_step, init, slen, period, lw)
            assert last, "deeper stacks need the un-normalised FFN output"
            y = _ffn_final(x2, mods, mod_base, rows_per_mod, lw["n2"], lw["w1b"], lw["w3b"], lw["w2b"], fn)
            new.append((y, st))
        (yp, st_p), (ys, _) = new
        states.append(st_p.reshape(batch, 2, GLA_HEADS, GLA_DK, GLA_DV))
    y_prompt = yp.reshape(batch, seq, D_MODEL)
    y_sample = ys.reshape(dec_batch, dec_seq, D_MODEL)
    return (y_prompt, y_sample, jnp.stack(states, axis=1))
```

```python
import functools
import math

import numpy as np
import jax
import jax.numpy as jnp
from jax import lax
from jax.experimental import pallas as pl
from jax.experimental.pallas import tpu as pltpu

F32 = jnp.float32
BF16 = jnp.bfloat16

D_MODEL = 1024
D_FF = 2816
N_MOD = 9
EPS = 1e-6
GLA_HEADS = 4
GLA_DK = 64
GLA_DV = 128
QK_W = GLA_HEADS * GLA_DK
GLA_WIDTH = GLA_HEADS * GLA_DV
CONV_WIDTH = 512
GLA_LOWRANK = 16
GLA_TAU = 16.0
GRID_W = 64
SPLITS = [QK_W, QK_W, GLA_WIDTH, GLA_WIDTH, GLA_LOWRANK, GLA_LOWRANK, CONV_WIDTH, CONV_WIDTH, CONV_WIDTH]

LANES = 128
ROW_GROUP = 16
CHUNK = 128
N_LEVELS = 7
N_HALF_LEVELS = 3
TILE_FFN = 512
FF_CHUNKS = ((0, 1024), (1024, 2048), (2048, D_FF))
TILE_MIX = 256
LR_PAD = LANES
QKV_W = 2 * QK_W + GLA_WIDTH
VMEM_LIMIT_FFN = 52 * 1024 * 1024
VMEM_LIMIT_MIX = 48 * 1024 * 1024
LOG2E = math.log2(math.e)

_NT = (((1,), (1,)), ((), ()))


def _dot(a, b):
    return jnp.dot(a, b, preferred_element_type=F32)


def _dot_nt(a, b):
    return lax.dot_general(a, b, _NT, preferred_element_type=F32)


def _silu(x):
    return x / (1.0 + jnp.exp(-x))


def _rms(x):
    return x * lax.rsqrt(jnp.mean(x * x, axis=-1, keepdims=True) + EPS)


def _rms_mod(x, gain, mod_ref, base):
    return _rms(x) * gain * (1.0 + mod_ref[base + 1:base + 2, :]) + mod_ref[base:base + 1, :]


def _const_spec(shape):
    nd = len(shape)
    return pl.BlockSpec(shape, lambda *_: (0,) * nd, pipeline_mode=pl.Buffered(1))


def _ada_kernel(c_ref, w_ref, b_ref, o_ref):
    a = _silu(c_ref[...]).astype(BF16)
    o_ref[...] = _dot(a, w_ref[...].astype(BF16)) + b_ref[...]


def _ada_mod(cvec, w, b):
    rows = cvec.shape[0]
    n = w.shape[1]
    tn = D_MODEL
    return pl.pallas_call(
        _ada_kernel,
        grid=(n // tn,),
        in_specs=[
            pl.BlockSpec((rows, D_MODEL), lambda i: (0, 0)),
            pl.BlockSpec((D_MODEL, tn), lambda i: (0, i)),
            pl.BlockSpec((1, tn), lambda i: (0, i)),
        ],
        out_specs=pl.BlockSpec((rows, tn), lambda i: (0, i)),
        out_shape=jax.ShapeDtypeStruct((rows, n), F32),
        name="ada_mod",
    )(cvec, w, b.reshape(1, n))


def _ffn_core(x, mod_ref, base, norm_ref, w1_ref, w3_ref, w2_ref):
    h = _rms_mod(x, norm_ref[...], mod_ref, base).astype(BF16)
    acc = None
    for c0, c1 in FF_CHUNKS:
        a = _dot(h, w1_ref[:, c0:c1])
        b = _dot(h, w3_ref[:, c0:c1])
        part = _dot((_silu(a) * b).astype(BF16), w2_ref[c0:c1, :])
        acc = part if acc is None else acc + part
    return x + (0.5 * mod_ref[base + 2:base + 3, :]) * acc


def _ffn_proj_kernel(x_ref, mod_ref, n1_ref, w1_ref, w3_ref, w2_ref, n2_ref, wqkv_ref, wlr_ref, wkt_ref,
                     xo_ref, qkv_ref, lr_ref, kt_ref):
    y = _ffn_core(x_ref[...], mod_ref, 0, n1_ref, w1_ref, w3_ref, w2_ref)
    xo_ref[...] = y
    h2 = _rms_mod(y, n2_ref[...], mod_ref, 3).astype(BF16)
    qkv_ref[...] = _dot(h2, wqkv_ref[...]).astype(BF16)
    lr_ref[...] = _dot(h2, wlr_ref[...]).astype(BF16)
    kt_ref[...] = _dot_nt(wkt_ref[...], h2).astype(BF16)


def _ffn_final_kernel(x_ref, mod_ref, n1_ref, w1_ref, w3_ref, w2_ref, fn_ref, yo_ref):
    y = _ffn_core(x_ref[...], mod_ref, 6, n1_ref, w1_ref, w3_ref, w2_ref)
    yo_ref[...] = _rms(y) * fn_ref[...]


def _mod_spec(mod_base, rows_per_mod, tile):
    per = rows_per_mod // tile
    return pl.BlockSpec((None, N_MOD, D_MODEL), lambda i: (mod_base + i // per, 0, 0))


def _ffn_weight_specs():
    return [_const_spec((1, D_MODEL)), _const_spec((D_MODEL, D_FF)), _const_spec((D_MODEL, D_FF)),
            _const_spec((D_FF, D_MODEL))]


def _ffn_proj(x, mods, mod_base, rows_per_mod, lw):
    n = x.shape[0]
    tm = TILE_FFN
    row = lambda i: (i, 0)
    return pl.pallas_call(
        _ffn_proj_kernel,
        grid=(n // tm,),
        in_specs=[pl.BlockSpec((tm, D_MODEL), row), _mod_spec(mod_base, rows_per_mod, tm)]
        + _ffn_weight_specs()
        + [_const_spec((1, D_MODEL)), _const_spec((D_MODEL, QKV_W)), _const_spec((D_MODEL, LR_PAD)),
           _const_spec((QK_W, D_MODEL))],
        out_specs=[
            pl.BlockSpec((tm, D_MODEL), row),
            pl.BlockSpec((tm, QKV_W), row),
            pl.BlockSpec((tm, LR_PAD), row),
            pl.BlockSpec((QK_W, tm), lambda i: (0, i)),
        ],
        out_shape=[
            jax.ShapeDtypeStruct((n, D_MODEL), F32),
            jax.ShapeDtypeStruct((n, QKV_W), BF16),
            jax.ShapeDtypeStruct((n, LR_PAD), BF16),
            jax.ShapeDtypeStruct((QK_W, n), BF16),
        ],
        compiler_params=pltpu.CompilerParams(
            dimension_semantics=("arbitrary",), vmem_limit_bytes=VMEM_LIMIT_FFN),
        name="ffn1_proj",
    )(x, mods, lw["n1"], lw["w1a"], lw["w3a"], lw["w2a"], lw["n_mix"], lw["w_qkv"], lw["w_lr"], lw["wkt"])


def _ffn_final(x, mods, mod_base, rows_per_mod, lw, fn):
    n = x.shape[0]
    tm = TILE_FFN
    row = lambda i: (i, 0)
    return pl.pallas_call(
        _ffn_final_kernel,
        grid=(n // tm,),
        in_specs=[pl.BlockSpec((tm, D_MODEL), row), _mod_spec(mod_base, rows_per_mod, tm)]
        + _ffn_weight_specs() + [_const_spec((1, D_MODEL))],
        out_specs=pl.BlockSpec((tm, D_MODEL), row),
        out_shape=jax.ShapeDtypeStruct((n, D_MODEL), F32),
        compiler_params=pltpu.CompilerParams(
            dimension_semantics=("arbitrary",), vmem_limit_bytes=VMEM_LIMIT_FFN),
        name="ffn2_final",
    )(x, mods, lw["n2"], lw["w1b"], lw["w3b"], lw["w2b"], fn)


def _gla_tables():
    c = CHUNK
    t = np.arange(c)[:, None]
    u = np.arange(c)[None, :]
    dsum = np.zeros((2, N_LEVELS + 1, c, c), np.float32)
    mask = np.zeros((2, N_LEVELS + 1, c, c), np.float32)
    drest = np.zeros((2, c, c), np.float32)
    for lvl in range(N_LEVELS):
        m = c >> (lvl + 1)
        mid = (t // (2 * m)) * (2 * m) + m
        mid_u = (u // (2 * m)) * (2 * m) + m
        same = (t // (2 * m)) == (u // (2 * m))
        hi_t, hi_u = t >= mid, u >= mid_u
        dsum[0, lvl] = np.where(hi_t, (u >= mid) & (u <= t), (u > t) & (u < mid))
        mask[0, lvl] = same & hi_t & ~hi_u
        dsum[1, lvl] = np.where(hi_t, (u >= mid) & (u < t), (u >= t) & (u < mid))
        mask[1, lvl] = same & ~hi_t & hi_u
    dsum[0, N_LEVELS] = u <= t
    dsum[1, N_LEVELS] = u >= t
    mask[:, N_LEVELS] = t == u
    drest[0] = u > t
    drest[1] = u < t
    dsum = dsum.reshape(2, (N_LEVELS + 1) * c, c)
    dsum2 = np.concatenate([dsum, dsum], axis=2)
    rest_t = np.transpose(drest, (0, 2, 1))
    rest1 = np.concatenate([rest_t, np.ones_like(rest_t)], axis=2)
    rest2 = np.concatenate([rest1, rest1], axis=1)
    mask2 = np.concatenate([mask, mask], axis=2)
    return (jnp.asarray(dsum2, BF16), jnp.asarray(rest2, BF16), jnp.asarray(mask2, F32))


def _query_rows(lvl, d):
    if lvl >= N_HALF_LEVELS:
        return [(0, 2 * CHUNK)]
    m = CHUNK >> (lvl + 1)
    assert m % ROW_GROUP == 0
    lo = m if d == 0 else 0
    return [(hh * CHUNK + b0 + lo, hh * CHUNK + b0 + lo + m)
            for hh in range(2) for b0 in range(0, CHUNK, 2 * m)]


def _split_hi_lo(x):
    hi = x.astype(BF16)
    lo = (x - hi.astype(F32)).astype(BF16)
    return hi, lo


def _log2_decay(x):
    log_sigmoid = jnp.minimum(x, 0.0) - jnp.log(1.0 + jnp.exp(-jnp.abs(x)))
    return log_sigmoid * (LOG2E / GLA_TAU)


def _rows(x, ranges):
    parts = [x[r0:r1] for r0, r1 in ranges]
    return parts[0] if len(parts) == 1 else jnp.concatenate(parts, axis=0)


def _gla_chunk(d, q, k, v, kt, la_hl, lat_hl, dsum_ref, rest_ref, mask_ref, s_prev):
    e_all = jnp.exp2(_dot(dsum_ref[...], la_hl))
    r = _dot(lat_hl, rest_ref[...])
    k_rest_t = (kt * jnp.exp2(r[:, :CHUNK])).astype(BF16)
    gamma = jnp.exp2(r[:, CHUNK:])
    lane = lax.broadcasted_iota(jnp.int32, (CHUNK, LANES), 1)
    outs, upd = [], []
    for pr in range(GLA_HEADS // 2):
        ls = slice(LANES * pr, LANES * (pr + 1))
        qp, kp = q[:, ls], k[:, ls]
        qm = jnp.concatenate([jnp.where(lane < GLA_DK, qp, 0.0),
                              jnp.where(lane >= GLA_DK, qp, 0.0)], axis=0)
        diag = _dot_nt(qm.astype(BF16), kp.astype(BF16)) * mask_ref[N_LEVELS]
        groups = [diag[g0:g0 + ROW_GROUP] for g0 in range(0, 2 * CHUNK, ROW_GROUP)]
        for lvl in range(N_LEVELS):
            e = e_all[CHUNK * lvl:CHUNK * (lvl + 1), ls]
            ranges = _query_rows(lvl, d)
            lhs = (_rows(qm, ranges) * _rows(jnp.concatenate([e, e], axis=0), ranges)).astype(BF16)
            part = _dot_nt(lhs, (kp * e).astype(BF16))
            off = 0
            for r0, r1 in ranges:
                for g0 in range(r0, r1, ROW_GROUP):
                    groups[g0 // ROW_GROUP] += part[off:off + ROW_GROUP] * mask_ref[lvl, g0:g0 + ROW_GROUP, :]
                    off += ROW_GROUP
        scores = jnp.concatenate(groups, axis=0).astype(BF16)
        e = e_all[CHUNK * N_LEVELS:, ls]
        q_in = (qm * jnp.concatenate([e, e], axis=0)).astype(BF16)
        o_inter = _dot(q_in, s_prev[LANES * pr:LANES * (pr + 1), :].astype(BF16))
        for hh in range(2):
            h = 2 * pr + hh
            vh = v[:, GLA_DV * h:GLA_DV * (h + 1)]
            rows = slice(CHUNK * hh, CHUNK * (hh + 1))
            outs.append(o_inter[rows] + _dot(scores[rows], vh))
            upd.append(_dot(k_rest_t[GLA_DK * h:GLA_DK * (h + 1), :], vh))
    s_new = gamma * s_prev + jnp.concatenate(upd, axis=0)
    return outs, s_new


def _mixer_kernel(*refs, nt, period, has_s0):
    (qk_ref, v_ref, lr_ref, kt_ref, x_ref, mod_ref) = refs[:6]
    refs = refs[6:]
    if has_s0:
        s0_ref, refs = refs[0], refs[1:]
    (wd_ref, bd_ref, wdt_ref, bdt_ref, dsum_ref, rest_ref, mask_ref,
     nm_ref, wg_ref, wc_ref, gn_ref, cw_ref, wout_ref,
     xo_ref, st_ref, o_ref, s_ref) = refs
    p = pl.program_id(1)
    j = pl.program_id(2)
    n_chunks = TILE_MIX // CHUNK

    def direction(d):
        @pl.when(j == 0)
        def _():
            s_ref[...] = s0_ref[d] if has_s0 else jnp.zeros(s_ref.shape, F32)

        tile = j if d == 0 else nt - 1 - j
        lr = lr_ref[...]
        la_hi, la_lo = _split_hi_lo(_log2_decay(_dot(lr, wd_ref[...]) + bd_ref[...]))
        lat_hi, lat_lo = _split_hi_lo(_log2_decay(_dot_nt(wdt_ref[...], lr) + bdt_ref[...]))
        q_all = qk_ref[:, :QK_W].astype(F32) * (GLA_DK ** -0.5)
        k_all = qk_ref[:, QK_W:].astype(F32)
        kt_all = kt_ref[...].astype(F32)
        s = s_ref[...]
        for c in (range(n_chunks) if d == 0 else reversed(range(n_chunks))):
            rows = slice(CHUNK * c, CHUNK * (c + 1))
            la_hl = jnp.concatenate([la_hi[rows], la_lo[rows]], axis=0)
            lat_hl = jnp.concatenate([lat_hi[:, rows], lat_lo[:, rows]], axis=1)
            outs, s = _gla_chunk(d, q_all[rows], k_all[rows], v_ref[rows, :], kt_all[:, rows],
                                 la_hl, lat_hl, dsum_ref, rest_ref, mask_ref, s)
            base = pl.multiple_of(tile * TILE_MIX + CHUNK * c, CHUNK)
            for h in range(GLA_HEADS):
                cols = slice(GLA_DV * h, GLA_DV * (h + 1))
                if d == 0:
                    o_ref[pl.ds(base, CHUNK), cols] = outs[h]
                else:
                    o_ref[pl.ds(base, CHUNK), cols] += outs[h]
        s_ref[...] = s

        @pl.when(j == nt - 1)
        def _():
            st_ref[d] = s

        if d == 1:
            x = x_ref[...]
            h2 = _rms_mod(x, nm_ref[...], mod_ref, 3).astype(BF16)
            base = pl.multiple_of(tile * TILE_MIX, TILE_MIX)
            o = o_ref[pl.ds(base, TILE_MIX), :]
            g = _dot(h2, wg_ref[...])
            parts = []
            for h in range(GLA_HEADS):
                cols = slice(GLA_DV * h, GLA_DV * (h + 1))
                parts.append(_rms(o[:, cols]) * gn_ref[:, cols] * _silu(g[:, cols]))
            conv = _dot(h2, wc_ref[...])
            cb = conv[:, :CONV_WIDTH]
            u = conv[:, CONV_WIDTH:2 * CONV_WIDTH] * conv[:, 2 * CONV_WIDTH:]
            pos = lax.broadcasted_iota(jnp.int32, u.shape, 0) & (period - 1)
            u_prev = jnp.where(pos == 0, 0.0, pltpu.roll(u, 1, 0))
            u_next = jnp.where(pos == period - 1, 0.0, pltpu.roll(u, TILE_MIX - 1, 0))
            cu = cw_ref[0:1, :] * u_prev + cw_ref[1:2, :] * u + cw_ref[2:3, :] * u_next
            parts.append(cb * cu)
            mixed = jnp.concatenate(parts, axis=-1).astype(BF16)
            xo_ref[...] = x + mod_ref[5:6, :] * _dot(mixed, wout_ref[...])

    @pl.when(p == 0)
    def _():
        direction(0)

    @pl.when(p == 1)
    def _():
        direction(1)


def _mixer(qkv, lr, kt, x, mods, mod_base, mod_step, s0, seq_len, period, lw):
    n = x.shape[0]
    batch = n // seq_len
    t = TILE_MIX
    nt = seq_len // t
    assert period & (period - 1) == 0 and t % period == 0

    def tile(p, j):
        return j + p * (nt - 1 - 2 * j)

    def late(p, j):
        return (nt - 1) - p * j

    in_specs = [
        pl.BlockSpec((t, 2 * QK_W), lambda b, p, j: (b * nt + tile(p, j), 0)),
        pl.BlockSpec((t, GLA_WIDTH), lambda b, p, j: (b * nt + tile(p, j), 1)),
        pl.BlockSpec((t, LR_PAD), lambda b, p, j: (b * nt + tile(p, j), 0)),
        pl.BlockSpec((QK_W, t), lambda b, p, j: (0, b * nt + tile(p, j))),
        pl.BlockSpec((t, D_MODEL), lambda b, p, j: (b * nt + late(p, j), 0)),
        pl.BlockSpec((None, N_MOD, D_MODEL), lambda b, p, j: (mod_base + mod_step * b, 0, 0)),
    ]
    args = [qkv, qkv, lr, kt, x, mods]
    if s0 is not None:
        in_specs.append(pl.BlockSpec((None, 2, QK_W, GLA_DV), lambda b, p, j: (b, 0, 0, 0)))
        args.append(s0)
    by_dir = lambda shape: pl.BlockSpec((None,) + shape, lambda b, p, j: (p,) + (0,) * len(shape))
    dsum, rest, mask = lw["tables"]
    in_specs += [
        by_dir((LR_PAD, QK_W)), by_dir((1, QK_W)), by_dir((QK_W, LR_PAD)), by_dir((QK_W, t)),
        by_dir(dsum.shape[1:]), by_dir(rest.shape[1:]), by_dir(mask.shape[1:]),
        _const_spec((1, D_MODEL)), _const_spec((D_MODEL, GLA_WIDTH)), _const_spec((D_MODEL, 3 * CONV_WIDTH)),
        _const_spec((1, GLA_WIDTH)), _const_spec((3, CONV_WIDTH)), _const_spec((D_MODEL, D_MODEL)),
    ]
    args += [lw["wd"], lw["bd"], lw["wdt"], lw["bdt"], dsum, rest, mask,
             lw["n_mix"], lw["w_g"], lw["w_conv"], lw["gla_norm"], lw["conv_w"], lw["w_out"]]
    return pl.pallas_call(
        functools.partial(_mixer_kernel, nt=nt, period=period, has_s0=s0 is not None),
        grid=(batch, 2, nt),
        in_specs=in_specs,
        out_specs=[
            pl.BlockSpec((t, D_MODEL), lambda b, p, j: (b * nt + late(p, j), 0)),
            pl.BlockSpec((None, 2, QK_W, GLA_DV), lambda b, p, j: (b, 0, 0, 0)),
        ],
        out_shape=[
            jax.ShapeDtypeStruct((n, D_MODEL), F32),
            jax.ShapeDtypeStruct((batch, 2, QK_W, GLA_DV), F32),
        ],
        scratch_shapes=[
            pltpu.VMEM((seq_len, GLA_WIDTH), F32),
            pltpu.VMEM((QK_W, GLA_DV), F32),
        ],
        compiler_params=pltpu.CompilerParams(
            dimension_semantics=("arbitrary", "arbitrary", "arbitrary"),
            vmem_limit_bytes=VMEM_LIMIT_MIX),
        name="mixer",
    )(*args)


def _prep_layer(l, norm_ffn1, w1_ffn1, w3_ffn1, w2_ffn1, norm_mix, w_in, w_decay, b_decay, gla_norm,
                conv_w, w_out, norm_ffn2, w1_ffn2, w3_ffn2, w2_ffn2, tables):
    edges = [0] + list(np.cumsum(SPLITS))
    col = lambda a, b: w_in[l][:, edges[a]:edges[b]]
    w_lr = jnp.pad(col(4, 6), ((0, 0), (0, LR_PAD - 2 * GLA_LOWRANK)))
    wd = jnp.zeros((2, LR_PAD, QK_W), F32)
    wd = wd.at[0, :GLA_LOWRANK].set(w_decay[l, 0]).at[1, GLA_LOWRANK:2 * GLA_LOWRANK].set(w_decay[l, 1])
    bd = b_decay[l].reshape(2, 1, QK_W)
    return dict(
        n1=norm_ffn1[l].reshape(1, -1), w1a=w1_ffn1[l].astype(BF16), w3a=w3_ffn1[l].astype(BF16),
        w2a=w2_ffn1[l].astype(BF16),
        n_mix=norm_mix[l].reshape(1, -1), w_qkv=col(0, 3).astype(BF16), w_lr=w_lr.astype(BF16),
        wkt=col(1, 2).T.astype(BF16), w_g=col(3, 4).astype(BF16), w_conv=col(6, 9).astype(BF16),
        wd=wd.astype(BF16), bd=bd, wdt=jnp.transpose(wd, (0, 2, 1)).astype(BF16),
        bdt=jnp.broadcast_to(jnp.transpose(bd, (0, 2, 1)), (2, QK_W, TILE_MIX)),
        gla_norm=gla_norm[l].reshape(1, -1), conv_w=conv_w[l], w_out=w_out[l].astype(BF16),
        n2=norm_ffn2[l].reshape(1, -1), w1b=w1_ffn2[l].astype(BF16), w3b=w3_ffn2[l].astype(BF16),
        w2b=w2_ffn2[l].astype(BF16), tables=tables)


def kernel(x_prompt, x_sample, state_gla, c, c_ctx, w_ada, b_ada, norm_ffn1, w1_ffn1, w3_ffn1, w2_ffn1,
           norm_mix, w_in, w_decay, b_decay, gla_norm, conv_w, w_out, norm_ffn2, w1_ffn2, w3_ffn2,
           w2_ffn2, final_norm):
    batch, seq, _ = x_prompt.shape
    dec_batch, dec_seq, _ = x_sample.shape
    depth = w_ada.shape[0]
    assert depth == 1, "the fused final norm assumes a single layer"
    l = 0
    xp = x_prompt.reshape(batch * seq, D_MODEL)
    xs = x_sample.reshape(dec_batch * dec_seq, D_MODEL)
    mod_rows = 8
    assert 1 + dec_batch <= mod_rows
    cvec = jnp.concatenate([c_ctx[None, :], c, jnp.zeros((mod_rows - 1 - dec_batch, D_MODEL), F32)], axis=0)
    fn = final_norm.reshape(1, -1)
    lw = _prep_layer(l, norm_ffn1, w1_ffn1, w3_ffn1, w2_ffn1, norm_mix, w_in, w_decay, b_decay,
                     gla_norm, conv_w, w_out, norm_ffn2, w1_ffn2, w3_ffn2, w2_ffn2, _gla_tables())
    mods = _ada_mod(cvec, w_ada[l], b_ada[l]).reshape(mod_rows, N_MOD, D_MODEL)
    s0 = state_gla[:, l].reshape(dec_batch, 2, QK_W, GLA_DV)
    results = []
    for (x, mod_base, mod_step, rows_per_mod, init, slen, period) in (
            (xp, 0, 0, batch * seq, None, seq, seq),
            (xs, 1, 1, dec_seq, s0, dec_seq, GRID_W)):
        x1, qkv, lr, kt = _ffn_proj(x, mods, mod_base, rows_per_mod, lw)
        x2, st = _mixer(qkv, lr, kt, x1, mods, mod_base, mod_step, init, slen, period, lw)
        results.append((_ffn_final(x2, mods, mod_base, rows_per_mod, lw, fn), st))
    (yp, st_p), (ys, _) = results
    y_prompt = yp.reshape(batch, seq, D_MODEL)
    y_sample = ys.reshape(dec_batch, dec_seq, D_MODEL)
    new_state = st_p.reshape(batch, 1, 2, GLA_HEADS, GLA_DK, GLA_DV)
    return (y_prompt, y_sample, new_state)
```

```python
import functools
import math

import numpy as np
import jax
import jax.numpy as jnp
from jax import lax
from jax.experimental import pallas as pl
from jax.experimental.pallas import tpu as pltpu

F32 = jnp.float32
BF16 = jnp.bfloat16

D_MODEL = 1024
D_FF = 2816
N_MOD = 9
EPS = 1e-6
GLA_HEADS = 4
GLA_DK = 64
GLA_DV = 128
QK_W = GLA_HEADS * GLA_DK
GLA_WIDTH = GLA_HEADS * GLA_DV
CONV_WIDTH = 512
GLA_LOWRANK = 16
GLA_TAU = 16.0
GRID_W = 64
SPLITS = [QK_W, QK_W, GLA_WIDTH, GLA_WIDTH, GLA_LOWRANK, GLA_LOWRANK, CONV_WIDTH, CONV_WIDTH, CONV_WIDTH]

LANES = 128
ROW_GROUP = 16
CHUNK = 128
N_LEVELS = 7
N_HALF_LEVELS = 3
TILE_FFN = 512
FF_CHUNKS = ((0, 1024), (1024, 2048), (2048, D_FF))
TILE_MIX = 256
SCAN_LEAD = 0
LR_PAD = LANES
QKV_W = 2 * QK_W + GLA_WIDTH
VMEM_LIMIT_FFN = 56 * 1024 * 1024
VMEM_LIMIT_MIX = 48 * 1024 * 1024
LOG2E = math.log2(math.e)

_NT = (((1,), (1,)), ((), ()))


def _dot(a, b):
    return jnp.dot(a, b, preferred_element_type=F32)


def _dot_nt(a, b):
    return lax.dot_general(a, b, _NT, preferred_element_type=F32)


def _silu(x):
    return x / (1.0 + jnp.exp(-x))


def _rms(x):
    return x * lax.rsqrt(jnp.mean(x * x, axis=-1, keepdims=True) + EPS)


def _rms_mod(x, gain, mod_ref, base):
    return _rms(x) * gain * (1.0 + mod_ref[base + 1:base + 2, :]) + mod_ref[base:base + 1, :]


def _const_spec(shape):
    nd = len(shape)
    return pl.BlockSpec(shape, lambda *_: (0,) * nd, pipeline_mode=pl.Buffered(1))


def _ada_kernel(c_ref, w_ref, b_ref, o_ref):
    a = _silu(c_ref[...]).astype(BF16)
    o_ref[...] = _dot(a, w_ref[...].astype(BF16)) + b_ref[...]


def _ada_mod(cvec, w, b):
    rows = cvec.shape[0]
    n = w.shape[1]
    tn = D_MODEL
    return pl.pallas_call(
        _ada_kernel,
        grid=(n // tn,),
        in_specs=[
            pl.BlockSpec((rows, D_MODEL), lambda i: (0, 0)),
            pl.BlockSpec((D_MODEL, tn), lambda i: (0, i)),
            pl.BlockSpec((1, tn), lambda i: (0, i)),
        ],
        out_specs=pl.BlockSpec((rows, tn), lambda i: (0, i)),
        out_shape=jax.ShapeDtypeStruct((rows, n), F32),
        name="ada_mod",
    )(cvec, w, b.reshape(1, n))


def _ffn_core(x, mod_ref, base, norm_ref, w1_ref, w3_ref, w2_ref):
    h = _rms_mod(x, norm_ref[...], mod_ref, base).astype(BF16)
    acc = None
    for c0, c1 in FF_CHUNKS:
        a = _dot(h, w1_ref[:, c0:c1])
        b = _dot(h, w3_ref[:, c0:c1])
        part = _dot((_silu(a) * b).astype(BF16), w2_ref[c0:c1, :])
        acc = part if acc is None else acc + part
    return x + (0.5 * mod_ref[base + 2:base + 3, :]) * acc


def _ffn_proj_kernel(x_ref, mod_ref, n1_ref, w1_ref, w3_ref, w2_ref, n2_ref, wqkv_ref, wlr_ref, wkt_ref,
                     xo_ref, qkv_ref, lr_ref, kt_ref):
    y = _ffn_core(x_ref[...], mod_ref, 0, n1_ref, w1_ref, w3_ref, w2_ref)
    xo_ref[...] = y
    h2 = _rms_mod(y, n2_ref[...], mod_ref, 3).astype(BF16)
    qkv_ref[...] = _dot(h2, wqkv_ref[...]).astype(BF16)
    lr_ref[...] = _dot(h2, wlr_ref[...]).astype(BF16)
    kt_ref[...] = _dot_nt(wkt_ref[...], h2).astype(BF16)


def _mod_spec(mod_base, rows_per_mod, tile):
    per = rows_per_mod // tile
    return pl.BlockSpec((None, N_MOD, D_MODEL), lambda i: (mod_base + i // per, 0, 0))


def _ffn_weight_specs():
    return [_const_spec((1, D_MODEL)), _const_spec((D_MODEL, D_FF)), _const_spec((D_MODEL, D_FF)),
            _const_spec((D_FF, D_MODEL))]


def _ffn_proj(x, mods, mod_base, rows_per_mod, lw):
    n = x.shape[0]
    tm = TILE_FFN
    row = lambda i: (i, 0)
    return pl.pallas_call(
        _ffn_proj_kernel,
        grid=(n // tm,),
        in_specs=[pl.BlockSpec((tm, D_MODEL), row), _mod_spec(mod_base, rows_per_mod, tm)]
        + _ffn_weight_specs()
        + [_const_spec((1, D_MODEL)), _const_spec((D_MODEL, QKV_W)), _const_spec((D_MODEL, LR_PAD)),
           _const_spec((QK_W, D_MODEL))],
        out_specs=[
            pl.BlockSpec((tm, D_MODEL), row),
            pl.BlockSpec((tm, QKV_W), row),
            pl.BlockSpec((tm, LR_PAD), row),
            pl.BlockSpec((QK_W, tm), lambda i: (0, i)),
        ],
        out_shape=[
            jax.ShapeDtypeStruct((n, D_MODEL), F32),
            jax.ShapeDtypeStruct((n, QKV_W), BF16),
            jax.ShapeDtypeStruct((n, LR_PAD), BF16),
            jax.ShapeDtypeStruct((QK_W, n), BF16),
        ],
        compiler_params=pltpu.CompilerParams(
            dimension_semantics=("arbitrary",), vmem_limit_bytes=VMEM_LIMIT_FFN),
        name="ffn1_proj",
    )(x, mods, lw["n1"], lw["w1a"], lw["w3a"], lw["w2a"], lw["n_mix"], lw["w_qkv"], lw["w_lr"], lw["wkt"])


def _gla_tables():
    c = CHUNK
    t = np.arange(c)[:, None]
    u = np.arange(c)[None, :]
    dsum = np.zeros((2, N_LEVELS + 1, c, c), np.float32)
    mask = np.zeros((2, N_LEVELS + 1, c, c), np.float32)
    drest = np.zeros((2, c, c), np.float32)
    for lvl in range(N_LEVELS):
        m = c >> (lvl + 1)
        mid = (t // (2 * m)) * (2 * m) + m
        mid_u = (u // (2 * m)) * (2 * m) + m
        same = (t // (2 * m)) == (u // (2 * m))
        hi_t, hi_u = t >= mid, u >= mid_u
        dsum[0, lvl] = np.where(hi_t, (u >= mid) & (u <= t), (u > t) & (u < mid))
        mask[0, lvl] = same & hi_t & ~hi_u
        dsum[1, lvl] = np.where(hi_t, (u >= mid) & (u < t), (u >= t) & (u < mid))
        mask[1, lvl] = same & ~hi_t & hi_u
    dsum[0, N_LEVELS] = u <= t
    dsum[1, N_LEVELS] = u >= t
    mask[:, N_LEVELS] = t == u
    drest[0] = u > t
    drest[1] = u < t
    dsum = dsum.reshape(2, (N_LEVELS + 1) * c, c)
    dsum2 = np.concatenate([dsum, dsum], axis=2)
    rest_t = np.transpose(drest, (0, 2, 1))
    rest1 = np.concatenate([rest_t, np.ones_like(rest_t)], axis=2)
    rest2 = np.concatenate([rest1, rest1], axis=1)
    mask2 = np.concatenate([mask, mask], axis=2)
    return (jnp.asarray(dsum2, BF16), jnp.asarray(rest2, BF16), jnp.asarray(mask2, F32))


def _query_rows(lvl, d):
    if lvl >= N_HALF_LEVELS:
        return [(0, 2 * CHUNK)]
    m = CHUNK >> (lvl + 1)
    assert m % ROW_GROUP == 0
    lo = m if d == 0 else 0
    return [(hh * CHUNK + b0 + lo, hh * CHUNK + b0 + lo + m)
            for hh in range(2) for b0 in range(0, CHUNK, 2 * m)]


def _split_hi_lo(x):
    hi = x.astype(BF16)
    lo = (x - hi.astype(F32)).astype(BF16)
    return hi, lo


def _log2_decay(x):
    log_sigmoid = jnp.minimum(x, 0.0) - jnp.log(1.0 + jnp.exp(-jnp.abs(x)))
    return log_sigmoid * (LOG2E / GLA_TAU)


def _rows(x, ranges):
    parts = [x[r0:r1] for r0, r1 in ranges]
    return parts[0] if len(parts) == 1 else jnp.concatenate(parts, axis=0)


def _gla_chunk(d, q, k, v, kt, la_hl, lat_hl, dsum_ref, rest_ref, mask_ref, state, outs):
    e_all = jnp.exp2(_dot(dsum_ref[...], la_hl))
    yield
    r = _dot(lat_hl, rest_ref[...])
    k_rest_t = (kt * jnp.exp2(r[:, :CHUNK])).astype(BF16)
    gamma = jnp.exp2(r[:, CHUNK:])
    lane = lax.broadcasted_iota(jnp.int32, (CHUNK, LANES), 1)
    upd = []
    for pr in range(GLA_HEADS // 2):
        ls = slice(LANES * pr, LANES * (pr + 1))
        qp, kp = q[:, ls], k[:, ls]
        qm = jnp.concatenate([jnp.where(lane < GLA_DK, qp, 0.0),
                              jnp.where(lane >= GLA_DK, qp, 0.0)], axis=0)
        diag = _dot_nt(qm.astype(BF16), kp.astype(BF16)) * mask_ref[N_LEVELS]
        groups = [diag[g0:g0 + ROW_GROUP] for g0 in range(0, 2 * CHUNK, ROW_GROUP)]
        yield
        for lvl in range(N_LEVELS):
            e = e_all[CHUNK * lvl:CHUNK * (lvl + 1), ls]
            ranges = _query_rows(lvl, d)
            lhs = (_rows(qm, ranges) * _rows(jnp.concatenate([e, e], axis=0), ranges)).astype(BF16)
            part = _dot_nt(lhs, (kp * e).astype(BF16))
            off = 0
            for r0, r1 in ranges:
                for g0 in range(r0, r1, ROW_GROUP):
                    groups[g0 // ROW_GROUP] += part[off:off + ROW_GROUP] * mask_ref[lvl, g0:g0 + ROW_GROUP, :]
                    off += ROW_GROUP
            yield
        scores = jnp.concatenate(groups, axis=0).astype(BF16)
        e = e_all[CHUNK * N_LEVELS:, ls]
        q_in = (qm * jnp.concatenate([e, e], axis=0)).astype(BF16)
        o_inter = _dot(q_in, state[0][LANES * pr:LANES * (pr + 1), :].astype(BF16))
        for hh in range(2):
            h = 2 * pr + hh
            vh = v[:, GLA_DV * h:GLA_DV * (h + 1)]
            rows = slice(CHUNK * hh, CHUNK * (hh + 1))
            outs.append(o_inter[rows] + _dot(scores[rows], vh))
            upd.append(_dot(k_rest_t[GLA_DK * h:GLA_DK * (h + 1), :], vh))
        yield
    state[0] = gamma * state[0] + jnp.concatenate(upd, axis=0)


def _interleave(generators, lead):
    active = list(generators)
    for _ in range(lead):
        next(active[0])
    while active:
        for g in list(active):
            try:
                next(g)
            except StopIteration:
                active.remove(g)


def _gla_kernel(*refs, nt, has_s0):
    ins, refs = refs[:8], refs[8:]
    if has_s0:
        s0_ref, refs = refs[0], refs[1:]
    (wd_ref, bd_ref, wdt_ref, bdt_ref, dsum_ref, rest_ref, mask_ref,
     of_ref, ob_ref, st_ref, s_ref) = refs
    n_chunks = TILE_MIX // CHUNK

    @pl.when(pl.program_id(1) == 0)
    def _():
        s_ref[...] = s0_ref[...] if has_s0 else jnp.zeros(s_ref.shape, F32)

    def scan(d):
        qk_ref, v_ref, lr_ref, kt_ref = ins[4 * d:4 * d + 4]
        o_ref = (of_ref, ob_ref)[d]
        lr = lr_ref[...]
        la_hi, la_lo = _split_hi_lo(_log2_decay(_dot(lr, wd_ref[d]) + bd_ref[d]))
        lat_hi, lat_lo = _split_hi_lo(_log2_decay(_dot_nt(wdt_ref[d], lr) + bdt_ref[d]))
        q_all = qk_ref[:, :QK_W].astype(F32) * (GLA_DK ** -0.5)
        k_all = qk_ref[:, QK_W:].astype(F32)
        kt_all = kt_ref[...].astype(F32)
        state = [s_ref[d]]
        yield
        for c in (range(n_chunks) if d == 0 else reversed(range(n_chunks))):
            rows = slice(CHUNK * c, CHUNK * (c + 1))
            la_hl = jnp.concatenate([la_hi[rows], la_lo[rows]], axis=0)
            lat_hl = jnp.concatenate([lat_hi[:, rows], lat_lo[:, rows]], axis=1)
            outs = []
            yield from _gla_chunk(d, q_all[rows], k_all[rows], v_ref[rows, :], kt_all[:, rows], la_hl, lat_hl,
                                  dsum_ref.at[d], rest_ref.at[d], mask_ref.at[d], state, outs)
            for h in range(GLA_HEADS):
                o_ref[rows, GLA_DV * h:GLA_DV * (h + 1)] = outs[h]
        s_ref[d] = state[0]
        st_ref[d] = state[0]

    _interleave([scan(0), scan(1)], lead=SCAN_LEAD)


def _gla(qkv, lr, kt, s0, seq_len, lw):
    n = qkv.shape[0]
    batch = n // seq_len
    t = TILE_MIX
    nt = seq_len // t
    tile = (lambda b, j: b * nt + j, lambda b, j: b * nt + nt - 1 - j)
    in_specs, args = [], []
    for d in range(2):
        in_specs += [
            pl.BlockSpec((t, 2 * QK_W), lambda b, j, d=d: (tile[d](b, j), 0)),
            pl.BlockSpec((t, GLA_WIDTH), lambda b, j, d=d: (tile[d](b, j), 1)),
            pl.BlockSpec((t, LR_PAD), lambda b, j, d=d: (tile[d](b, j), 0)),
            pl.BlockSpec((QK_W, t), lambda b, j, d=d: (0, tile[d](b, j))),
        ]
        args += [qkv, qkv, lr, kt]
    state_spec = pl.BlockSpec((None, 2, QK_W, GLA_DV), lambda b, j: (b, 0, 0, 0))
    if s0 is not None:
        in_specs.append(state_spec)
        args.append(s0)
    tables = [lw["wd"], lw["bd"], lw["wdt"], lw["bdt"]] + list(lw["tables"])
    in_specs += [_const_spec(a.shape) for a in tables]
    args += tables
    return pl.pallas_call(
        functools.partial(_gla_kernel, nt=nt, has_s0=s0 is not None),
        grid=(batch, nt),
        in_specs=in_specs,
        out_specs=[
            pl.BlockSpec((t, GLA_WIDTH), lambda b, j: (tile[0](b, j), 0)),
            pl.BlockSpec((t, GLA_WIDTH), lambda b, j: (tile[1](b, j), 0)),
            state_spec,
        ],
        out_shape=[
            jax.ShapeDtypeStruct((n, GLA_WIDTH), F32),
            jax.ShapeDtypeStruct((n, GLA_WIDTH), F32),
            jax.ShapeDtypeStruct((batch, 2, QK_W, GLA_DV), F32),
        ],
        scratch_shapes=[pltpu.VMEM((2, QK_W, GLA_DV), F32)],
        compiler_params=pltpu.CompilerParams(
            dimension_semantics=("arbitrary", "arbitrary"), vmem_limit_bytes=VMEM_LIMIT_MIX),
        name="gla",
    )(*args)


def _mix_ffn_kernel(x_ref, of_ref, ob_ref, mod_ref, nm_ref, wg_ref, wc_ref, gn_ref, cw_ref, wout_ref,
                    n2_ref, w1_ref, w3_ref, w2_ref, fn_ref, yo_ref, *, period):
    x = x_ref[...]
    h2 = _rms_mod(x, nm_ref[...], mod_ref, 3).astype(BF16)
    o = of_ref[...] + ob_ref[...]
    g = _dot(h2, wg_ref[...])
    parts = []
    for h in range(GLA_HEADS):
        cols = slice(GLA_DV * h, GLA_DV * (h + 1))
        parts.append(_rms(o[:, cols]) * gn_ref[:, cols] * _silu(g[:, cols]))
    conv = _dot(h2, wc_ref[...])
    cb = conv[:, :CONV_WIDTH]
    u = conv[:, CONV_WIDTH:2 * CONV_WIDTH] * conv[:, 2 * CONV_WIDTH:]
    pos = lax.broadcasted_iota(jnp.int32, u.shape, 0) & (period - 1)
    u_prev = jnp.where(pos == 0, 0.0, pltpu.roll(u, 1, 0))
    u_next = jnp.where(pos == period - 1, 0.0, pltpu.roll(u, u.shape[0] - 1, 0))
    cu = cw_ref[0:1, :] * u_prev + cw_ref[1:2, :] * u + cw_ref[2:3, :] * u_next
    parts.append(cb * cu)
    mixed = jnp.concatenate(parts, axis=-1).astype(BF16)
    x2 = x + mod_ref[5:6, :] * _dot(mixed, wout_ref[...])
    y = _ffn_core(x2, mod_ref, 6, n2_ref, w1_ref, w3_ref, w2_ref)
    yo_ref[...] = _rms(y) * fn_ref[...]


def _mix_ffn(x, o_fwd, o_bwd, mods, mod_base, rows_per_mod, period, lw, fn):
    n = x.shape[0]
    tm = TILE_FFN
    assert period & (period - 1) == 0 and tm % period == 0
    row = lambda i: (i, 0)
    return pl.pallas_call(
        functools.partial(_mix_ffn_kernel, period=period),
        grid=(n // tm,),
        in_specs=[pl.BlockSpec((tm, D_MODEL), row), pl.BlockSpec((tm, GLA_WIDTH), row),
                  pl.BlockSpec((tm, GLA_WIDTH), row), _mod_spec(mod_base, rows_per_mod, tm),
                  _const_spec((1, D_MODEL)), _const_spec((D_MODEL, GLA_WIDTH)),
                  _const_spec((D_MODEL, 3 * CONV_WIDTH)), _const_spec((1, GLA_WIDTH)),
                  _const_spec((3, CONV_WIDTH)), _const_spec((D_MODEL, D_MODEL))]
        + _ffn_weight_specs() + [_const_spec((1, D_MODEL))],
        out_specs=pl.BlockSpec((tm, D_MODEL), row),
        out_shape=jax.ShapeDtypeStruct((n, D_MODEL), F32),
        compiler_params=pltpu.CompilerParams(
            dimension_semantics=("arbitrary",), vmem_limit_bytes=VMEM_LIMIT_FFN),
        name="mix_ffn2",
    )(x, o_fwd, o_bwd, mods, lw["n_mix"], lw["w_g"], lw["w_conv"], lw["gla_norm"], lw["conv_w"], lw["w_out"],
      lw["n2"], lw["w1b"], lw["w3b"], lw["w2b"], fn)


def _prep_layer(l, norm_ffn1, w1_ffn1, w3_ffn1, w2_ffn1, norm_mix, w_in, w_decay, b_decay, gla_norm,
                conv_w, w_out, norm_ffn2, w1_ffn2, w3_ffn2, w2_ffn2, tables):
    edges = [0] + list(np.cumsum(SPLITS))
    col = lambda a, b: w_in[l][:, edges[a]:edges[b]]
    w_lr = jnp.pad(col(4, 6), ((0, 0), (0, LR_PAD - 2 * GLA_LOWRANK)))
    wd = jnp.zeros((2, LR_PAD, QK_W), F32)
    wd = wd.at[0, :GLA_LOWRANK].set(w_decay[l, 0]).at[1, GLA_LOWRANK:2 * GLA_LOWRANK].set(w_decay[l, 1])
    bd = b_decay[l].reshape(2, 1, QK_W)
    return dict(
        n1=norm_ffn1[l].reshape(1, -1), w1a=w1_ffn1[l].astype(BF16), w3a=w3_ffn1[l].astype(BF16),
        w2a=w2_ffn1[l].astype(BF16),
        n_mix=norm_mix[l].reshape(1, -1), w_qkv=col(0, 3).astype(BF16), w_lr=w_lr.astype(BF16),
        wkt=col(1, 2).T.astype(BF16), w_g=col(3, 4).astype(BF16), w_conv=col(6, 9).astype(BF16),
        wd=wd.astype(BF16), bd=bd, wdt=jnp.transpose(wd, (0, 2, 1)).astype(BF16),
        bdt=jnp.broadcast_to(jnp.transpose(bd, (0, 2, 1)), (2, QK_W, TILE_MIX)),
        gla_norm=gla_norm[l].reshape(1, -1), conv_w=conv_w[l], w_out=w_out[l].astype(BF16),
        n2=norm_ffn2[l].reshape(1, -1), w1b=w1_ffn2[l].astype(BF16), w3b=w3_ffn2[l].astype(BF16),
        w2b=w2_ffn2[l].astype(BF16), tables=tables)


def kernel(x_prompt, x_sample, state_gla, c, c_ctx, w_ada, b_ada, norm_ffn1, w1_ffn1, w3_ffn1, w2_ffn1,
           norm_mix, w_in, w_decay, b_decay, gla_norm, conv_w, w_out, norm_ffn2, w1_ffn2, w3_ffn2,
           w2_ffn2, final_norm):
    batch, seq, _ = x_prompt.shape
    dec_batch, dec_seq, _ = x_sample.shape
    depth = w_ada.shape[0]
    assert depth == 1, "the fused final norm assumes a single layer"
    l = 0
    xp = x_prompt.reshape(batch * seq, D_MODEL)
    xs = x_sample.reshape(dec_batch * dec_seq, D_MODEL)
    mod_rows = 8
    assert 1 + dec_batch <= mod_rows
    cvec = jnp.concatenate([c_ctx[None, :], c, jnp.zeros((mod_rows - 1 - dec_batch, D_MODEL), F32)], axis=0)
    fn = final_norm.reshape(1, -1)
    lw = _prep_layer(l, norm_ffn1, w1_ffn1, w3_ffn1, w2_ffn1, norm_mix, w_in, w_decay, b_decay,
                     gla_norm, conv_w, w_out, norm_ffn2, w1_ffn2, w3_ffn2, w2_ffn2, _gla_tables())
    mods = _ada_mod(cvec, w_ada[l], b_ada[l]).reshape(mod_rows, N_MOD, D_MODEL)
    s0 = state_gla[:, l].reshape(dec_batch, 2, QK_W, GLA_DV)
    results = []
    for (x, mod_base, rows_per_mod, init, slen, period) in (
            (xp, 0, batch * seq, None, seq, seq),
            (xs, 1, dec_seq, s0, dec_seq, GRID_W)):
        x1, qkv, lr, kt = _ffn_proj(x, mods, mod_base, rows_per_mod, lw)
        o_fwd, o_bwd, st = _gla(qkv, lr, kt, init, slen, lw)
        results.append((_mix_ffn(x1, o_fwd, o_bwd, mods, mod_base, rows_per_mod, period, lw, fn), st))
    (yp, st_p), (ys, _) = results
    y_prompt = yp.reshape(batch, seq, D_MODEL)
    y_sample = ys.reshape(dec_batch, dec_seq, D_MODEL)
    new_state = st_p.reshape(batch, 1, 2, GLA_HEADS, GLA_DK, GLA_DV)
    return (y_prompt, y_sample, new_state)
```

```python
import functools
import math

import numpy as np
import jax
import jax.numpy as jnp
from jax import lax
from jax.experimental import pallas as pl
from jax.experimental.pallas import tpu as pltpu

F32 = jnp.float32
BF16 = jnp.bfloat16

D_MODEL = 1024
D_FF = 2816
N_MOD = 9
EPS = 1e-6
GLA_HEADS = 4
GLA_DK = 64
GLA_DV = 128
QK_W = GLA_HEADS * GLA_DK
GLA_WIDTH = GLA_HEADS * GLA_DV
CONV_WIDTH = 512
GLA_LOWRANK = 16
GLA_TAU = 16.0
GRID_W = 64
SPLITS = [QK_W, QK_W, GLA_WIDTH, GLA_WIDTH, GLA_LOWRANK, GLA_LOWRANK, CONV_WIDTH, CONV_WIDTH, CONV_WIDTH]

LANES = 128
ROW_GROUP = 16
CHUNK = 128
N_LEVELS = 7
N_HALF_LEVELS = 3
TILE_FFN = 512
FF_CHUNKS = ((0, 1024), (1024, 2048), (2048, D_FF))
TILE_MIX = 256
SCAN_LEAD = 0
LR_PAD = LANES
QKV_W = 2 * QK_W + GLA_WIDTH
VMEM_LIMIT_FFN = 56 * 1024 * 1024
VMEM_LIMIT_MIX = 48 * 1024 * 1024
LOG2E = math.log2(math.e)

_NT = (((1,), (1,)), ((), ()))


def _dot(a, b):
    return jnp.dot(a, b, preferred_element_type=F32)


def _dot_nt(a, b):
    return lax.dot_general(a, b, _NT, preferred_element_type=F32)


def _silu(x):
    return x / (1.0 + jnp.exp(-x))


def _rms(x):
    return x * lax.rsqrt(jnp.mean(x * x, axis=-1, keepdims=True) + EPS)


def _rms_mod(x, gain, mod_ref, base):
    return _rms(x) * gain * (1.0 + mod_ref[base + 1:base + 2, :]) + mod_ref[base:base + 1, :]


def _const_spec(shape):
    nd = len(shape)
    return pl.BlockSpec(shape, lambda *_: (0,) * nd, pipeline_mode=pl.Buffered(1))


def _ada_kernel(c_ref, w_ref, b_ref, o_ref):
    a = _silu(c_ref[...]).astype(BF16)
    o_ref[...] = _dot(a, w_ref[...].astype(BF16)) + b_ref[...]


def _ada_mod(cvec, w, b):
    rows = cvec.shape[0]
    n = w.shape[1]
    tn = D_MODEL
    return pl.pallas_call(
        _ada_kernel,
        grid=(n // tn,),
        in_specs=[
            pl.BlockSpec((rows, D_MODEL), lambda i: (0, 0)),
            pl.BlockSpec((D_MODEL, tn), lambda i: (0, i)),
            pl.BlockSpec((1, tn), lambda i: (0, i)),
        ],
        out_specs=pl.BlockSpec((rows, tn), lambda i: (0, i)),
        out_shape=jax.ShapeDtypeStruct((rows, n), F32),
        name="ada_mod",
    )(cvec, w, b.reshape(1, n))


def _ffn_core(x, mod_ref, base, norm_ref, w1_ref, w3_ref, w2_ref):
    h = _rms_mod(x, norm_ref[...], mod_ref, base).astype(BF16)
    acc = None
    for c0, c1 in FF_CHUNKS:
        a = _dot(h, w1_ref[:, c0:c1])
        b = _dot(h, w3_ref[:, c0:c1])
        part = _dot((_silu(a) * b).astype(BF16), w2_ref[c0:c1, :])
        acc = part if acc is None else acc + part
    return x + (0.5 * mod_ref[base + 2:base + 3, :]) * acc


def _ffn_proj_kernel(x_ref, mod_ref, n1_ref, w1_ref, w3_ref, w2_ref, n2_ref, wqkv_ref, wlr_ref, wkt_ref,
                     xo_ref, qkv_ref, lr_ref, kt_ref):
    y = _ffn_core(x_ref[...], mod_ref, 0, n1_ref, w1_ref, w3_ref, w2_ref)
    xo_ref[...] = y
    h2 = _rms_mod(y, n2_ref[...], mod_ref, 3).astype(BF16)
    qkv_ref[...] = _dot(h2, wqkv_ref[...]).astype(BF16)
    lr_ref[...] = _dot(h2, wlr_ref[...]).astype(BF16)
    kt_ref[...] = _dot_nt(wkt_ref[...], h2).astype(BF16)


def _mod_spec(mod_base, rows_per_mod, tile):
    per = rows_per_mod // tile
    return pl.BlockSpec((None, N_MOD, D_MODEL), lambda i: (mod_base + i // per, 0, 0))


def _ffn_weight_specs():
    return [_const_spec((1, D_MODEL)), _const_spec((D_MODEL, D_FF)), _const_spec((D_MODEL, D_FF)),
            _const_spec((D_FF, D_MODEL))]


def _ffn_proj(x, mods, mod_base, rows_per_mod, lw):
    n = x.shape[0]
    tm = TILE_FFN
    row = lambda i: (i, 0)
    return pl.pallas_call(
        _ffn_proj_kernel,
        grid=(n // tm,),
        in_specs=[pl.BlockSpec((tm, D_MODEL), row), _mod_spec(mod_base, rows_per_mod, tm)]
        + _ffn_weight_specs()
        + [_const_spec((1, D_MODEL)), _const_spec((D_MODEL, QKV_W)), _const_spec((D_MODEL, LR_PAD)),
           _const_spec((QK_W, D_MODEL))],
        out_specs=[
            pl.BlockSpec((tm, D_MODEL), row),
            pl.BlockSpec((tm, QKV_W), row),
            pl.BlockSpec((tm, LR_PAD), row),
            pl.BlockSpec((QK_W, tm), lambda i: (0, i)),
        ],
        out_shape=[
            jax.ShapeDtypeStruct((n, D_MODEL), F32),
            jax.ShapeDtypeStruct((n, QKV_W), BF16),
            jax.ShapeDtypeStruct((n, LR_PAD), BF16),
            jax.ShapeDtypeStruct((QK_W, n), BF16),
        ],
        compiler_params=pltpu.CompilerParams(
            dimension_semantics=("arbitrary",), vmem_limit_bytes=VMEM_LIMIT_FFN),
        name="ffn1_proj",
    )(x, mods, lw["n1"], lw["w1a"], lw["w3a"], lw["w2a"], lw["n_mix"], lw["w_qkv"], lw["w_lr"], lw["wkt"])


def _gla_tables():
    c = CHUNK
    t = np.arange(c)[:, None]
    u = np.arange(c)[None, :]
    dsum = np.zeros((2, N_LEVELS + 1, c, c), np.float32)
    mask = np.zeros((2, N_LEVELS + 1, c, c), np.float32)
    drest = np.zeros((2, c, c), np.float32)
    for lvl in range(N_LEVELS):
        m = c >> (lvl + 1)
        mid = (t // (2 * m)) * (2 * m) + m
        mid_u = (u // (2 * m)) * (2 * m) + m
        same = (t // (2 * m)) == (u // (2 * m))
        hi_t, hi_u = t >= mid, u >= mid_u
        dsum[0, lvl] = np.where(hi_t, (u >= mid) & (u <= t), (u > t) & (u < mid))
        mask[0, lvl] = same & hi_t & ~hi_u
        dsum[1, lvl] = np.where(hi_t, (u >= mid) & (u < t), (u >= t) & (u < mid))
        mask[1, lvl] = same & ~hi_t & hi_u
    dsum[0, N_LEVELS] = u <= t
    dsum[1, N_LEVELS] = u >= t
    mask[:, N_LEVELS] = t == u
    drest[0] = u > t
    drest[1] = u < t
    dsum = dsum.reshape(2, (N_LEVELS + 1) * c, c)
    dsum2 = np.concatenate([dsum, dsum], axis=2)
    rest_t = np.transpose(drest, (0, 2, 1))
    rest1 = np.concatenate([rest_t, np.ones_like(rest_t)], axis=2)
    rest2 = np.concatenate([rest1, rest1], axis=1)
    mask2 = np.concatenate([mask, mask], axis=2)
    return (jnp.asarray(dsum2, BF16), jnp.asarray(rest2, BF16), jnp.asarray(mask2, F32))


def _query_rows(lvl, d):
    if lvl >= N_HALF_LEVELS:
        return [(0, 2 * CHUNK)]
    m = CHUNK >> (lvl + 1)
    assert m % ROW_GROUP == 0
    lo = m if d == 0 else 0
    return [(hh * CHUNK + b0 + lo, hh * CHUNK + b0 + lo + m)
            for hh in range(2) for b0 in range(0, CHUNK, 2 * m)]


def _split_hi_lo(x):
    hi = x.astype(BF16)
    lo = (x - hi.astype(F32)).astype(BF16)
    return hi, lo


def _log2_decay(x):
    log_sigmoid = jnp.minimum(x, 0.0) - jnp.log(1.0 + jnp.exp(-jnp.abs(x)))
    return log_sigmoid * (LOG2E / GLA_TAU)


def _rows(x, ranges):
    parts = [x[r0:r1] for r0, r1 in ranges]
    return parts[0] if len(parts) == 1 else jnp.concatenate(parts, axis=0)


def _gla_chunk(d, q, k, v, kt, la_hl, lat_hl, dsum_ref, rest_ref, mask_ref, state, outs):
    e_all = jnp.exp2(_dot(dsum_ref[...], la_hl)).astype(BF16)
    yield
    r = _dot(lat_hl, rest_ref[...])
    k_rest_t = kt * jnp.exp2(r[:, :CHUNK]).astype(BF16)
    gamma = jnp.exp2(r[:, CHUNK:])
    lane = lax.broadcasted_iota(jnp.int32, (CHUNK, LANES), 1)
    zero = jnp.zeros((), BF16)
    upd = []
    for pr in range(GLA_HEADS // 2):
        ls = slice(LANES * pr, LANES * (pr + 1))
        qp, kp = q[:, ls], k[:, ls]
        qm = jnp.concatenate([jnp.where(lane < GLA_DK, qp, zero),
                              jnp.where(lane >= GLA_DK, qp, zero)], axis=0)
        diag = _dot_nt(qm, kp) * mask_ref[N_LEVELS]
        groups = [diag[g0:g0 + ROW_GROUP] for g0 in range(0, 2 * CHUNK, ROW_GROUP)]
        yield
        for lvl in range(N_LEVELS):
            e = e_all[CHUNK * lvl:CHUNK * (lvl + 1), ls]
            ranges = _query_rows(lvl, d)
            lhs = _rows(qm, ranges) * _rows(jnp.concatenate([e, e], axis=0), ranges)
            part = _dot_nt(lhs, kp * e)
            off = 0
            for r0, r1 in ranges:
                for g0 in range(r0, r1, ROW_GROUP):
                    groups[g0 // ROW_GROUP] += part[off:off + ROW_GROUP] * mask_ref[lvl, g0:g0 + ROW_GROUP, :]
                    off += ROW_GROUP
            yield
        scores = jnp.concatenate(groups, axis=0).astype(BF16)
        e = e_all[CHUNK * N_LEVELS:, ls]
        q_in = qm * jnp.concatenate([e, e], axis=0)
        o_inter = _dot(q_in, state[0][LANES * pr:LANES * (pr + 1), :].astype(BF16))
        for hh in range(2):
            h = 2 * pr + hh
            vh = v[:, GLA_DV * h:GLA_DV * (h + 1)]
            rows = slice(CHUNK * hh, CHUNK * (hh + 1))
            outs.append(o_inter[rows] + _dot(scores[rows], vh))
            upd.append(_dot(k_rest_t[GLA_DK * h:GLA_DK * (h + 1), :], vh))
        yield
    state[0] = gamma * state[0] + jnp.concatenate(upd, axis=0)


def _interleave(generators, lead):
    active = list(generators)
    for _ in range(lead):
        next(active[0])
    while active:
        for g in list(active):
            try:
                next(g)
            except StopIteration:
                active.remove(g)


def _gla_kernel(*refs, nt, has_s0):
    ins, refs = refs[:8], refs[8:]
    if has_s0:
        s0_ref, refs = refs[0], refs[1:]
    (wd_ref, bd_ref, wdt_ref, bdt_ref, dsum_ref, rest_ref, mask_ref,
     of_ref, ob_ref, st_ref, s_ref) = refs
    n_chunks = TILE_MIX // CHUNK

    @pl.when(pl.program_id(1) == 0)
    def _():
        s_ref[...] = s0_ref[...] if has_s0 else jnp.zeros(s_ref.shape, F32)

    def scan(d):
        qk_ref, v_ref, lr_ref, kt_ref = ins[4 * d:4 * d + 4]
        o_ref = (of_ref, ob_ref)[d]
        lr = lr_ref[...]
        la_hi, la_lo = _split_hi_lo(_log2_decay(_dot(lr, wd_ref[d]) + bd_ref[d]))
        lat_hi, lat_lo = _split_hi_lo(_log2_decay(_dot_nt(wdt_ref[d], lr) + bdt_ref[d]))
        q_all = qk_ref[:, :QK_W] * jnp.asarray(GLA_DK ** -0.5, BF16)
        k_all = qk_ref[:, QK_W:]
        kt_all = kt_ref[...]
        state = [s_ref[d]]
        yield
        for c in (range(n_chunks) if d == 0 else reversed(range(n_chunks))):
            rows = slice(CHUNK * c, CHUNK * (c + 1))
            la_hl = jnp.concatenate([la_hi[rows], la_lo[rows]], axis=0)
            lat_hl = jnp.concatenate([lat_hi[:, rows], lat_lo[:, rows]], axis=1)
            outs = []
            yield from _gla_chunk(d, q_all[rows], k_all[rows], v_ref[rows, :], kt_all[:, rows], la_hl, lat_hl,
                                  dsum_ref.at[d], rest_ref.at[d], mask_ref.at[d], state, outs)
            for h in range(GLA_HEADS):
                o_ref[rows, GLA_DV * h:GLA_DV * (h + 1)] = outs[h]
        s_ref[d] = state[0]
        st_ref[d] = state[0]

    _interleave([scan(0), scan(1)], lead=SCAN_LEAD)


def _gla(qkv, lr, kt, s0, seq_len, lw):
    n = qkv.shape[0]
    batch = n // seq_len
    t = TILE_MIX
    nt = seq_len // t
    tile = (lambda b, j: b * nt + j, lambda b, j: b * nt + nt - 1 - j)
    in_specs, args = [], []
    for d in range(2):
        in_specs += [
            pl.BlockSpec((t, 2 * QK_W), lambda b, j, d=d: (tile[d](b, j), 0)),
            pl.BlockSpec((t, GLA_WIDTH), lambda b, j, d=d: (tile[d](b, j), 1)),
            pl.BlockSpec((t, LR_PAD), lambda b, j, d=d: (tile[d](b, j), 0)),
            pl.BlockSpec((QK_W, t), lambda b, j, d=d: (0, tile[d](b, j))),
        ]
        args += [qkv, qkv, lr, kt]
    state_spec = pl.BlockSpec((None, 2, QK_W, GLA_DV), lambda b, j: (b, 0, 0, 0))
    if s0 is not None:
        in_specs.append(state_spec)
        args.append(s0)
    tables = [lw["wd"], lw["bd"], lw["wdt"], lw["bdt"]] + list(lw["tables"])
    in_specs += [_const_spec(a.shape) for a in tables]
    args += tables
    return pl.pallas_call(
        functools.partial(_gla_kernel, nt=nt, has_s0=s0 is not None),
        grid=(batch, nt),
        in_specs=in_specs,
        out_specs=[
            pl.BlockSpec((t, GLA_WIDTH), lambda b, j: (tile[0](b, j), 0)),
            pl.BlockSpec((t, GLA_WIDTH), lambda b, j: (tile[1](b, j), 0)),
            state_spec,
        ],
        out_shape=[
            jax.ShapeDtypeStruct((n, GLA_WIDTH), F32),
            jax.ShapeDtypeStruct((n, GLA_WIDTH), F32),
            jax.ShapeDtypeStruct((batch, 2, QK_W, GLA_DV), F32),
        ],
        scratch_shapes=[pltpu.VMEM((2, QK_W, GLA_DV), F32)],
        compiler_params=pltpu.CompilerParams(
            dimension_semantics=("arbitrary", "arbitrary"), vmem_limit_bytes=VMEM_LIMIT_MIX),
        name="gla",
    )(*args)


def _mix_ffn_kernel(x_ref, of_ref, ob_ref, mod_ref, nm_ref, wg_ref, wc_ref, gn_ref, cw_ref, wout_ref,
                    n2_ref, w1_ref, w3_ref, w2_ref, fn_ref, yo_ref, *, period):
    x = x_ref[...]
    h2 = _rms_mod(x, nm_ref[...], mod_ref, 3).astype(BF16)
    o = of_ref[...] + ob_ref[...]
    g = _dot(h2, wg_ref[...])
    parts = []
    for h in range(GLA_HEADS):
        cols = slice(GLA_DV * h, GLA_DV * (h + 1))
        parts.append(_rms(o[:, cols]) * gn_ref[:, cols] * _silu(g[:, cols]))
    conv = _dot(h2, wc_ref[...])
    cb = conv[:, :CONV_WIDTH]
    u = conv[:, CONV_WIDTH:2 * CONV_WIDTH] * conv[:, 2 * CONV_WIDTH:]
    pos = lax.broadcasted_iota(jnp.int32, u.shape, 0) & (period - 1)
    u_prev = jnp.where(pos == 0, 0.0, pltpu.roll(u, 1, 0))
    u_next = jnp.where(pos == period - 1, 0.0, pltpu.roll(u, u.shape[0] - 1, 0))
    cu = cw_ref[0:1, :] * u_prev + cw_ref[1:2, :] * u + cw_ref[2:3, :] * u_next
    parts.append(cb * cu)
    mixed = jnp.concatenate(parts, axis=-1).astype(BF16)
    x2 = x + mod_ref[5:6, :] * _dot(mixed, wout_ref[...])
    y = _ffn_core(x2, mod_ref, 6, n2_ref, w1_ref, w3_ref, w2_ref)
    yo_ref[...] = _rms(y) * fn_ref[...]


def _mix_ffn(x, o_fwd, o_bwd, mods, mod_base, rows_per_mod, period, lw, fn):
    n = x.shape[0]
    tm = TILE_FFN
    assert period & (period - 1) == 0 and tm % period == 0
    row = lambda i: (i, 0)
    return pl.pallas_call(
        functools.partial(_mix_ffn_kernel, period=period),
        grid=(n // tm,),
        in_specs=[pl.BlockSpec((tm, D_MODEL), row), pl.BlockSpec((tm, GLA_WIDTH), row),
                  pl.BlockSpec((tm, GLA_WIDTH), row), _mod_spec(mod_base, rows_per_mod, tm),
                  _const_spec((1, D_MODEL)), _const_spec((D_MODEL, GLA_WIDTH)),
                  _const_spec((D_MODEL, 3 * CONV_WIDTH)), _const_spec((1, GLA_WIDTH)),
                  _const_spec((3, CONV_WIDTH)), _const_spec((D_MODEL, D_MODEL))]
        + _ffn_weight_specs() + [_const_spec((1, D_MODEL))],
        out_specs=pl.BlockSpec((tm, D_MODEL), row),
        out_shape=jax.ShapeDtypeStruct((n, D_MODEL), F32),
        compiler_params=pltpu.CompilerParams(
            dimension_semantics=("arbitrary",), vmem_limit_bytes=VMEM_LIMIT_FFN),
        name="mix_ffn2",
    )(x, o_fwd, o_bwd, mods, lw["n_mix"], lw["w_g"], lw["w_conv"], lw["gla_norm"], lw["conv_w"], lw["w_out"],
      lw["n2"], lw["w1b"], lw["w3b"], lw["w2b"], fn)


def _prep_layer(l, norm_ffn1, w1_ffn1, w3_ffn1, w2_ffn1, norm_mix, w_in, w_decay, b_decay, gla_norm,
                conv_w, w_out, norm_ffn2, w1_ffn2, w3_ffn2, w2_ffn2, tables):
    edges = [0] + list(np.cumsum(SPLITS))
    col = lambda a, b: w_in[l][:, edges[a]:edges[b]]
    w_lr = jnp.pad(col(4, 6), ((0, 0), (0, LR_PAD - 2 * GLA_LOWRANK)))
    wd = jnp.zeros((2, LR_PAD, QK_W), F32)
    wd = wd.at[0, :GLA_LOWRANK].set(w_decay[l, 0]).at[1, GLA_LOWRANK:2 * GLA_LOWRANK].set(w_decay[l, 1])
    bd = b_decay[l].reshape(2, 1, QK_W)
    return dict(
        n1=norm_ffn1[l].reshape(1, -1), w1a=w1_ffn1[l].astype(BF16), w3a=w3_ffn1[l].astype(BF16),
        w2a=w2_ffn1[l].astype(BF16),
        n_mix=norm_mix[l].reshape(1, -1), w_qkv=col(0, 3).astype(BF16), w_lr=w_lr.astype(BF16),
        wkt=col(1, 2).T.astype(BF16), w_g=col(3, 4).astype(BF16), w_conv=col(6, 9).astype(BF16),
        wd=wd.astype(BF16), bd=bd, wdt=jnp.transpose(wd, (0, 2, 1)).astype(BF16),
        bdt=jnp.broadcast_to(jnp.transpose(bd, (0, 2, 1)), (2, QK_W, TILE_MIX)),
        gla_norm=gla_norm[l].reshape(1, -1), conv_w=conv_w[l], w_out=w_out[l].astype(BF16),
        n2=norm_ffn2[l].reshape(1, -1), w1b=w1_ffn2[l].astype(BF16), w3b=w3_ffn2[l].astype(BF16),
        w2b=w2_ffn2[l].astype(BF16), tables=tables)


def kernel(x_prompt, x_sample, state_gla, c, c_ctx, w_ada, b_ada, norm_ffn1, w1_ffn1, w3_ffn1, w2_ffn1,
           norm_mix, w_in, w_decay, b_decay, gla_norm, conv_w, w_out, norm_ffn2, w1_ffn2, w3_ffn2,
           w2_ffn2, final_norm):
    batch, seq, _ = x_prompt.shape
    dec_batch, dec_seq, _ = x_sample.shape
    depth = w_ada.shape[0]
    assert depth == 1, "the fused final norm assumes a single layer"
    l = 0
    xp = x_prompt.reshape(batch * seq, D_MODEL)
    xs = x_sample.reshape(dec_batch * dec_seq, D_MODEL)
    mod_rows = 8
    assert 1 + dec_batch <= mod_rows
    cvec = jnp.concatenate([c_ctx[None, :], c, jnp.zeros((mod_rows - 1 - dec_batch, D_MODEL), F32)], axis=0)
    fn = final_norm.reshape(1, -1)
    lw = _prep_layer(l, norm_ffn1, w1_ffn1, w3_ffn1, w2_ffn1, norm_mix, w_in, w_decay, b_decay,
                     gla_norm, conv_w, w_out, norm_ffn2, w1_ffn2, w3_ffn2, w2_ffn2, _gla_tables())
    mods = _ada_mod(cvec, w_ada[l], b_ada[l]).reshape(mod_rows, N_MOD, D_MODEL)
    s0 = state_gla[:, l].reshape(dec_batch, 2, QK_W, GLA_DV)
    results = []
    for (x, mod_base, rows_per_mod, init, slen, period) in (
            (xp, 0, batch * seq, None, seq, seq),
            (xs, 1, dec_seq, s0, dec_seq, GRID_W)):
        x1, qkv, lr, kt = _ffn_proj(x, mods, mod_base, rows_per_mod, lw)
        o_fwd, o_bwd, st = _gla(qkv, lr, kt, init, slen, lw)
        results.append((_mix_ffn(x1, o_fwd, o_bwd, mods, mod_base, rows_per_mod, period, lw, fn), st))
    (yp, st_p), (ys, _) = results
    y_prompt = yp.reshape(batch, seq, D_MODEL)
    y_sample = ys.reshape(dec_batch, dec_seq, D_MODEL)
    new_state = st_p.reshape(batch, 1, 2, GLA_HEADS, GLA_DK, GLA_DV)
    return (y_prompt, y_sample, new_state)
```

```python
import functools
import math

import numpy as np
import jax
import jax.numpy as jnp
from jax import lax
from jax.experimental import pallas as pl
from jax.experimental.pallas import tpu as pltpu

F32 = jnp.float32
BF16 = jnp.bfloat16

D_MODEL = 1024
D_FF = 2816
N_MOD = 9
EPS = 1e-6
GLA_HEADS = 4
GLA_DK = 64
GLA_DV = 128
QK_W = GLA_HEADS * GLA_DK
GLA_WIDTH = GLA_HEADS * GLA_DV
CONV_WIDTH = 512
GLA_LOWRANK = 16
GLA_TAU = 16.0
GRID_W = 64
SPLITS = [QK_W, QK_W, GLA_WIDTH, GLA_WIDTH, GLA_LOWRANK, GLA_LOWRANK, CONV_WIDTH, CONV_WIDTH, CONV_WIDTH]

LANES = 128
ROW_GROUP = 16
CHUNK = 128
N_LEVELS = 7
N_HALF_LEVELS = 3
TILE_FFN = 512
FF_CHUNKS = ((0, 1024), (1024, 2048), (2048, D_FF))
TILE_MIX = 256
FFN_LEAD = 1
LR_PAD = LANES
QKV_W = 2 * QK_W + GLA_WIDTH
VMEM_LIMIT_FFN = 56 * 1024 * 1024
VMEM_LIMIT_MIX = 48 * 1024 * 1024
LOG2E = math.log2(math.e)

_NT = (((1,), (1,)), ((), ()))


def _dot(a, b):
    return jnp.dot(a, b, preferred_element_type=F32)


def _dot_nt(a, b):
    return lax.dot_general(a, b, _NT, preferred_element_type=F32)


def _silu(x):
    return x / (1.0 + jnp.exp(-x))


def _rms(x):
    return x * lax.rsqrt(jnp.mean(x * x, axis=-1, keepdims=True) + EPS)


def _rms_mod(x, gain, mod_ref, base):
    return _rms(x) * gain * (1.0 + mod_ref[base + 1:base + 2, :]) + mod_ref[base:base + 1, :]


def _const_spec(shape):
    nd = len(shape)
    return pl.BlockSpec(shape, lambda *_: (0,) * nd, pipeline_mode=pl.Buffered(1))


def _ada_kernel(c_ref, w_ref, b_ref, o_ref):
    a = _silu(c_ref[...]).astype(BF16)
    o_ref[...] = _dot(a, w_ref[...].astype(BF16)) + b_ref[...]


def _ada_mod(cvec, w, b):
    rows = cvec.shape[0]
    n = w.shape[1]
    tn = D_MODEL
    return pl.pallas_call(
        _ada_kernel,
        grid=(n // tn,),
        in_specs=[
            pl.BlockSpec((rows, D_MODEL), lambda i: (0, 0)),
            pl.BlockSpec((D_MODEL, tn), lambda i: (0, i)),
            pl.BlockSpec((1, tn), lambda i: (0, i)),
        ],
        out_specs=pl.BlockSpec((rows, tn), lambda i: (0, i)),
        out_shape=jax.ShapeDtypeStruct((rows, n), F32),
        name="ada_mod",
    )(cvec, w, b.reshape(1, n))


def _interleave(generators, lead=0):
    active = list(generators)
    for _ in range(lead):
        next(active[0])
    while active:
        for g in list(active):
            try:
                next(g)
            except StopIteration:
                active.remove(g)


def _ffn_core(x, mod_ref, base, norm_ref, w1_ref, w3_ref, w2_ref, out):
    h = _rms_mod(x, norm_ref[...], mod_ref, base).astype(BF16)
    yield
    acc = None
    for c0, c1 in FF_CHUNKS:
        a = _dot(h, w1_ref[:, c0:c1])
        b = _dot(h, w3_ref[:, c0:c1])
        yield
        part = _dot((_silu(a) * b).astype(BF16), w2_ref[c0:c1, :])
        acc = part if acc is None else acc + part
        yield
    out.append(x + (0.5 * mod_ref[base + 2:base + 3, :]) * acc)


def _row_halves(n):
    half = n // 2
    return [slice(0, half), slice(half, n)]


def _ffn_proj_kernel(x_ref, mod_ref, n1_ref, w1_ref, w3_ref, w2_ref, n2_ref, wqkv_ref, wlr_ref, wkt_ref,
                     xo_ref, qkv_ref, lr_ref, kt_ref):
    def half(rows):
        out = []
        yield from _ffn_core(x_ref[rows, :], mod_ref, 0, n1_ref, w1_ref, w3_ref, w2_ref, out)
        y = out[0]
        xo_ref[rows, :] = y
        h2 = _rms_mod(y, n2_ref[...], mod_ref, 3).astype(BF16)
        yield
        qkv_ref[rows, :] = _dot(h2, wqkv_ref[...]).astype(BF16)
        yield
        lr_ref[rows, :] = _dot(h2, wlr_ref[...]).astype(BF16)
        kt_ref[:, rows] = _dot_nt(wkt_ref[...], h2).astype(BF16)

    _interleave([half(rows) for rows in _row_halves(x_ref.shape[0])], lead=FFN_LEAD)


def _mod_spec(mod_base, rows_per_mod, tile):
    per = rows_per_mod // tile
    return pl.BlockSpec((None, N_MOD, D_MODEL), lambda i: (mod_base + i // per, 0, 0))


def _ffn_weight_specs():
    return [_const_spec((1, D_MODEL)), _const_spec((D_MODEL, D_FF)), _const_spec((D_MODEL, D_FF)),
            _const_spec((D_FF, D_MODEL))]


def _ffn_proj(x, mods, mod_base, rows_per_mod, lw):
    n = x.shape[0]
    tm = TILE_FFN
    row = lambda i: (i, 0)
    return pl.pallas_call(
        _ffn_proj_kernel,
        grid=(n // tm,),
        in_specs=[pl.BlockSpec((tm, D_MODEL), row), _mod_spec(mod_base, rows_per_mod, tm)]
        + _ffn_weight_specs()
        + [_const_spec((1, D_MODEL)), _const_spec((D_MODEL, QKV_W)), _const_spec((D_MODEL, LR_PAD)),
           _const_spec((QK_W, D_MODEL))],
        out_specs=[
            pl.BlockSpec((tm, D_MODEL), row),
            pl.BlockSpec((tm, QKV_W), row),
            pl.BlockSpec((tm, LR_PAD), row),
            pl.BlockSpec((QK_W, tm), lambda i: (0, i)),
        ],
        out_shape=[
            jax.ShapeDtypeStruct((n, D_MODEL), F32),
            jax.ShapeDtypeStruct((n, QKV_W), BF16),
            jax.ShapeDtypeStruct((n, LR_PAD), BF16),
            jax.ShapeDtypeStruct((QK_W, n), BF16),
        ],
        compiler_params=pltpu.CompilerParams(
            dimension_semantics=("arbitrary",), vmem_limit_bytes=VMEM_LIMIT_FFN),
        name="ffn1_proj",
    )(x, mods, lw["n1"], lw["w1a"], lw["w3a"], lw["w2a"], lw["n_mix"], lw["w_qkv"], lw["w_lr"], lw["wkt"])


def _gla_tables():
    c = CHUNK
    t = np.arange(c)[:, None]
    u = np.arange(c)[None, :]
    dsum = np.zeros((2, N_LEVELS + 1, c, c), np.float32)
    mask = np.zeros((2, N_LEVELS + 1, c, c), np.float32)
    drest = np.zeros((2, c, c), np.float32)
    for lvl in range(N_LEVELS):
        m = c >> (lvl + 1)
        mid = (t // (2 * m)) * (2 * m) + m
        mid_u = (u // (2 * m)) * (2 * m) + m
        same = (t // (2 * m)) == (u // (2 * m))
        hi_t, hi_u = t >= mid, u >= mid_u
        dsum[0, lvl] = np.where(hi_t, (u >= mid) & (u <= t), (u > t) & (u < mid))
        mask[0, lvl] = same & hi_t & ~hi_u
        dsum[1, lvl] = np.where(hi_t, (u >= mid) & (u < t), (u >= t) & (u < mid))
        mask[1, lvl] = same & ~hi_t & hi_u
    dsum[0, N_LEVELS] = u <= t
    dsum[1, N_LEVELS] = u >= t
    mask[:, N_LEVELS] = t == u
    drest[0] = u > t
    drest[1] = u < t
    dsum = dsum.reshape(2, (N_LEVELS + 1) * c, c)
    dsum2 = np.concatenate([dsum, dsum], axis=2)
    rest_t = np.transpose(drest, (0, 2, 1))
    rest1 = np.concatenate([rest_t, np.ones_like(rest_t)], axis=2)
    rest2 = np.concatenate([rest1, rest1], axis=1)
    mask2 = np.concatenate([mask, mask], axis=2)
    return (jnp.asarray(dsum2, BF16), jnp.asarray(rest2, BF16), jnp.asarray(mask2, F32))


def _query_rows(lvl, d):
    if lvl >= N_HALF_LEVELS:
        return [(0, 2 * CHUNK)]
    m = CHUNK >> (lvl + 1)
    assert m % ROW_GROUP == 0
    lo = m if d == 0 else 0
    return [(hh * CHUNK + b0 + lo, hh * CHUNK + b0 + lo + m)
            for hh in range(2) for b0 in range(0, CHUNK, 2 * m)]


def _split_hi_lo(x):
    hi = x.astype(BF16)
    lo = (x - hi.astype(F32)).astype(BF16)
    return hi, lo


def _log2_decay(x):
    log_sigmoid = jnp.minimum(x, 0.0) - jnp.log(1.0 + jnp.exp(-jnp.abs(x)))
    return log_sigmoid * (LOG2E / GLA_TAU)


def _rows(x, ranges):
    parts = [x[r0:r1] for r0, r1 in ranges]
    return parts[0] if len(parts) == 1 else jnp.concatenate(parts, axis=0)


def _gla_chunk(d, q, k, v, kt, la_hl, lat_hl, dsum_ref, rest_ref, mask_ref, state, outs):
    e_all = jnp.exp2(_dot(dsum_ref[...], la_hl)).astype(BF16)
    yield
    r = _dot(lat_hl, rest_ref[...])
    k_rest_t = kt * jnp.exp2(r[:, :CHUNK]).astype(BF16)
    gamma = jnp.exp2(r[:, CHUNK:])
    lane = lax.broadcasted_iota(jnp.int32, (CHUNK, LANES), 1)
    zero = jnp.zeros((), BF16)
    upd = []
    for pr in range(GLA_HEADS // 2):
        ls = slice(LANES * pr, LANES * (pr + 1))
        qp, kp = q[:, ls], k[:, ls]
        qm = jnp.concatenate([jnp.where(lane < GLA_DK, qp, zero),
                              jnp.where(lane >= GLA_DK, qp, zero)], axis=0)
        diag = _dot_nt(qm, kp) * mask_ref[N_LEVELS]
        groups = [diag[g0:g0 + ROW_GROUP] for g0 in range(0, 2 * CHUNK, ROW_GROUP)]
        yield
        for lvl in range(N_LEVELS):
            e = e_all[CHUNK * lvl:CHUNK * (lvl + 1), ls]
            ranges = _query_rows(lvl, d)
            lhs = _rows(qm, ranges) * _rows(jnp.concatenate([e, e], axis=0), ranges)
            part = _dot_nt(lhs, kp * e)
            off = 0
            for r0, r1 in ranges:
                for g0 in range(r0, r1, ROW_GROUP):
                    groups[g0 // ROW_GROUP] += part[off:off + ROW_GROUP] * mask_ref[lvl, g0:g0 + ROW_GROUP, :]
                    off += ROW_GROUP
            yield
        scores = jnp.concatenate(groups, axis=0).astype(BF16)
        e = e_all[CHUNK * N_LEVELS:, ls]
        q_in = qm * jnp.concatenate([e, e], axis=0)
        o_inter = _dot(q_in, state[0][LANES * pr:LANES * (pr + 1), :].astype(BF16))
        for hh in range(2):
            h = 2 * pr + hh
            vh = v[:, GLA_DV * h:GLA_DV * (h + 1)]
            rows = slice(CHUNK * hh, CHUNK * (hh + 1))
            outs.append(o_inter[rows] + _dot(scores[rows], vh))
            upd.append(_dot(k_rest_t[GLA_DK * h:GLA_DK * (h + 1), :], vh))
        yield
    state[0] = gamma * state[0] + jnp.concatenate(upd, axis=0)


def _gla_kernel(*refs, nt, has_s0):
    ins, refs = refs[:8], refs[8:]
    if has_s0:
        s0_ref, refs = refs[0], refs[1:]
    (wd_ref, bd_ref, wdt_ref, bdt_ref, dsum_ref, rest_ref, mask_ref,
     of_ref, ob_ref, st_ref, s_ref) = refs
    n_chunks = TILE_MIX // CHUNK

    @pl.when(pl.program_id(1) == 0)
    def _():
        s_ref[...] = s0_ref[...] if has_s0 else jnp.zeros(s_ref.shape, F32)

    def scan(d):
        qk_ref, v_ref, lr_ref, kt_ref = ins[4 * d:4 * d + 4]
        o_ref = (of_ref, ob_ref)[d]
        lr = lr_ref[...]
        la_hi, la_lo = _split_hi_lo(_log2_decay(_dot(lr, wd_ref[d]) + bd_ref[d]))
        lat_hi, lat_lo = _split_hi_lo(_log2_decay(_dot_nt(wdt_ref[d], lr) + bdt_ref[d]))
        q_all = qk_ref[:, :QK_W] * jnp.asarray(GLA_DK ** -0.5, BF16)
        k_all = qk_ref[:, QK_W:]
        kt_all = kt_ref[...]
        state = [s_ref[d]]
        yield
        for c in (range(n_chunks) if d == 0 else reversed(range(n_chunks))):
            rows = slice(CHUNK * c, CHUNK * (c + 1))
            la_hl = jnp.concatenate([la_hi[rows], la_lo[rows]], axis=0)
            lat_hl = jnp.concatenate([lat_hi[:, rows], lat_lo[:, rows]], axis=1)
            outs = []
            yield from _gla_chunk(d, q_all[rows], k_all[rows], v_ref[rows, :], kt_all[:, rows], la_hl, lat_hl,
                                  dsum_ref.at[d], rest_ref.at[d], mask_ref.at[d], state, outs)
            for h in range(GLA_HEADS):
                o_ref[rows, GLA_DV * h:GLA_DV * (h + 1)] = outs[h]
        s_ref[d] = state[0]
        st_ref[d] = state[0]

    _interleave([scan(0), scan(1)])


def _gla(qkv, lr, kt, s0, seq_len, lw):
    n = qkv.shape[0]
    batch = n // seq_len
    t = TILE_MIX
    nt = seq_len // t
    tile = (lambda b, j: b * nt + j, lambda b, j: b * nt + nt - 1 - j)
    in_specs, args = [], []
    for d in range(2):
        in_specs += [
            pl.BlockSpec((t, 2 * QK_W), lambda b, j, d=d: (tile[d](b, j), 0)),
            pl.BlockSpec((t, GLA_WIDTH), lambda b, j, d=d: (tile[d](b, j), 1)),
            pl.BlockSpec((t, LR_PAD), lambda b, j, d=d: (tile[d](b, j), 0)),
            pl.BlockSpec((QK_W, t), lambda b, j, d=d: (0, tile[d](b, j))),
        ]
        args += [qkv, qkv, lr, kt]
    state_spec = pl.BlockSpec((None, 2, QK_W, GLA_DV), lambda b, j: (b, 0, 0, 0))
    if s0 is not None:
        in_specs.append(state_spec)
        args.append(s0)
    tables = [lw["wd"], lw["bd"], lw["wdt"], lw["bdt"]] + list(lw["tables"])
    in_specs += [_const_spec(a.shape) for a in tables]
    args += tables
    return pl.pallas_call(
        functools.partial(_gla_kernel, nt=nt, has_s0=s0 is not None),
        grid=(batch, nt),
        in_specs=in_specs,
        out_specs=[
            pl.BlockSpec((t, GLA_WIDTH), lambda b, j: (tile[0](b, j), 0)),
            pl.BlockSpec((t, GLA_WIDTH), lambda b, j: (tile[1](b, j), 0)),
            state_spec,
        ],
        out_shape=[
            jax.ShapeDtypeStruct((n, GLA_WIDTH), F32),
            jax.ShapeDtypeStruct((n, GLA_WIDTH), F32),
            jax.ShapeDtypeStruct((batch, 2, QK_W, GLA_DV), F32),
        ],
        scratch_shapes=[pltpu.VMEM((2, QK_W, GLA_DV), F32)],
        compiler_params=pltpu.CompilerParams(
            dimension_semantics=("arbitrary", "arbitrary"), vmem_limit_bytes=VMEM_LIMIT_MIX),
        name="gla",
    )(*args)


def _mix_ffn_kernel(x_ref, of_ref, ob_ref, mod_ref, nm_ref, wg_ref, wc_ref, gn_ref, cw_ref, wout_ref,
                    n2_ref, w1_ref, w3_ref, w2_ref, fn_ref, yo_ref, *, period):
    def half(rows):
        x = x_ref[rows, :]
        h2 = _rms_mod(x, nm_ref[...], mod_ref, 3).astype(BF16)
        o = of_ref[rows, :] + ob_ref[rows, :]
        yield
        g = _dot(h2, wg_ref[...])
        parts = []
        for h in range(GLA_HEADS):
            cols = slice(GLA_DV * h, GLA_DV * (h + 1))
            parts.append(_rms(o[:, cols]) * gn_ref[:, cols] * _silu(g[:, cols]))
        yield
        conv = _dot(h2, wc_ref[...])
        cb = conv[:, :CONV_WIDTH]
        u = conv[:, CONV_WIDTH:2 * CONV_WIDTH] * conv[:, 2 * CONV_WIDTH:]
        pos = lax.broadcasted_iota(jnp.int32, u.shape, 0) & (period - 1)
        u_prev = jnp.where(pos == 0, 0.0, pltpu.roll(u, 1, 0))
        u_next = jnp.where(pos == period - 1, 0.0, pltpu.roll(u, u.shape[0] - 1, 0))
        cu = cw_ref[0:1, :] * u_prev + cw_ref[1:2, :] * u + cw_ref[2:3, :] * u_next
        parts.append(cb * cu)
        mixed = jnp.concatenate(parts, axis=-1).astype(BF16)
        yield
        x2 = x + mod_ref[5:6, :] * _dot(mixed, wout_ref[...])
        out = []
        yield from _ffn_core(x2, mod_ref, 6, n2_ref, w1_ref, w3_ref, w2_ref, out)
        yo_ref[rows, :] = _rms(out[0]) * fn_ref[...]

    _interleave([half(rows) for rows in _row_halves(x_ref.shape[0])], lead=FFN_LEAD)


def _mix_ffn(x, o_fwd, o_bwd, mods, mod_base, rows_per_mod, period, lw, fn):
    n = x.shape[0]
    tm = TILE_FFN
    assert period & (period - 1) == 0 and (tm // 2) % period == 0
    row = lambda i: (i, 0)
    return pl.pallas_call(
        functools.partial(_mix_ffn_kernel, period=period),
        grid=(n // tm,),
        in_specs=[pl.BlockSpec((tm, D_MODEL), row), pl.BlockSpec((tm, GLA_WIDTH), row),
                  pl.BlockSpec((tm, GLA_WIDTH), row), _mod_spec(mod_base, rows_per_mod, tm),
                  _const_spec((1, D_MODEL)), _const_spec((D_MODEL, GLA_WIDTH)),
                  _const_spec((D_MODEL, 3 * CONV_WIDTH)), _const_spec((1, GLA_WIDTH)),
                  _const_spec((3, CONV_WIDTH)), _const_spec((D_MODEL, D_MODEL))]
        + _ffn_weight_specs() + [_const_spec((1, D_MODEL))],
        out_specs=pl.BlockSpec((tm, D_MODEL), row),
        out_shape=jax.ShapeDtypeStruct((n, D_MODEL), F32),
        compiler_params=pltpu.CompilerParams(
            dimension_semantics=("arbitrary",), vmem_limit_bytes=VMEM_LIMIT_FFN),
        name="mix_ffn2",
    )(x, o_fwd, o_bwd, mods, lw["n_mix"], lw["w_g"], lw["w_conv"], lw["gla_norm"], lw["conv_w"], lw["w_out"],
      lw["n2"], lw["w1b"], lw["w3b"], lw["w2b"], fn)


def _prep_layer(l, norm_ffn1, w1_ffn1, w3_ffn1, w2_ffn1, norm_mix, w_in, w_decay, b_decay, gla_norm,
                conv_w, w_out, norm_ffn2, w1_ffn2, w3_ffn2, w2_ffn2, tables):
    edges = [0] + list(np.cumsum(SPLITS))
    col = lambda a, b: w_in[l][:, edges[a]:edges[b]]
    w_lr = jnp.pad(col(4, 6), ((0, 0), (0, LR_PAD - 2 * GLA_LOWRANK)))
    wd = jnp.zeros((2, LR_PAD, QK_W), F32)
    wd = wd.at[0, :GLA_LOWRANK].set(w_decay[l, 0]).at[1, GLA_LOWRANK:2 * GLA_LOWRANK].set(w_decay[l, 1])
    bd = b_decay[l].reshape(2, 1, QK_W)
    return dict(
        n1=norm_ffn1[l].reshape(1, -1), w1a=w1_ffn1[l].astype(BF16), w3a=w3_ffn1[l].astype(BF16),
        w2a=w2_ffn1[l].astype(BF16),
        n_mix=norm_mix[l].reshape(1, -1), w_qkv=col(0, 3).astype(BF16), w_lr=w_lr.astype(BF16),
        wkt=col(1, 2).T.astype(BF16), w_g=col(3, 4).astype(BF16), w_conv=col(6, 9).astype(BF16),
        wd=wd.astype(BF16), bd=bd, wdt=jnp.transpose(wd, (0, 2, 1)).astype(BF16),
        bdt=jnp.broadcast_to(jnp.transpose(bd, (0, 2, 1)), (2, QK_W, TILE_MIX)),
        gla_norm=gla_norm[l].reshape(1, -1), conv_w=conv_w[l], w_out=w_out[l].astype(BF16),
        n2=norm_ffn2[l].reshape(1, -1), w1b=w1_ffn2[l].astype(BF16), w3b=w3_ffn2[l].astype(BF16),
        w2b=w2_ffn2[l].astype(BF16), tables=tables)


def kernel(x_prompt, x_sample, state_gla, c, c_ctx, w_ada, b_ada, norm_ffn1, w1_ffn1, w3_ffn1, w2_ffn1,
           norm_mix, w_in, w_decay, b_decay, gla_norm, conv_w, w_out, norm_ffn2, w1_ffn2, w3_ffn2,
           w2_ffn2, final_norm):
    batch, seq, _ = x_prompt.shape
    dec_batch, dec_seq, _ = x_sample.shape
    depth = w_ada.shape[0]
    assert depth == 1, "the fused final norm assumes a single layer"
    l = 0
    xp = x_prompt.reshape(batch * seq, D_MODEL)
    xs = x_sample.reshape(dec_batch * dec_seq, D_MODEL)
    mod_rows = 8
    assert 1 + dec_batch <= mod_rows
    cvec = jnp.concatenate([c_ctx[None, :], c, jnp.zeros((mod_rows - 1 - dec_batch, D_MODEL), F32)], axis=0)
    fn = final_norm.reshape(1, -1)
    lw = _prep_layer(l, norm_ffn1, w1_ffn1, w3_ffn1, w2_ffn1, norm_mix, w_in, w_decay, b_decay,
                     gla_norm, conv_w, w_out, norm_ffn2, w1_ffn2, w3_ffn2, w2_ffn2, _gla_tables())
    mods = _ada_mod(cvec, w_ada[l], b_ada[l]).reshape(mod_rows, N_MOD, D_MODEL)
    s0 = state_gla[:, l].reshape(dec_batch, 2, QK_W, GLA_DV)
    results = []
    for (x, mod_base, rows_per_mod, init, slen, period) in (
            (xp, 0, batch * seq, None, seq, seq),
            (xs, 1, dec_seq, s0, dec_seq, GRID_W)):
        x1, qkv, lr, kt = _ffn_proj(x, mods, mod_base, rows_per_mod, lw)
        o_fwd, o_bwd, st = _gla(qkv, lr, kt, init, slen, lw)
        results.append((_mix_ffn(x1, o_fwd, o_bwd, mods, mod_base, rows_per_mod, period, lw, fn), st))
    (yp, st_p), (ys, _) = results
    y_prompt = yp.reshape(batch, seq, D_MODEL)
    y_sample = ys.reshape(dec_batch, dec_seq, D_MODEL)
    new_state = st_p.reshape(batch, 1, 2, GLA_HEADS, GLA_DK, GLA_DV)
    return (y_prompt, y_sample, new_state)
```

```python
import functools
import math

import numpy as np
import jax
import jax.numpy as jnp
from jax import lax
from jax.experimental import pallas as pl
from jax.experimental.pallas import tpu as pltpu

F32 = jnp.float32
BF16 = jnp.bfloat16

D_MODEL = 1024
D_FF = 2816
N_MOD = 9
EPS = 1e-6
GLA_HEADS = 4
GLA_DK = 64
GLA_DV = 128
QK_W = GLA_HEADS * GLA_DK
GLA_WIDTH = GLA_HEADS * GLA_DV
CONV_WIDTH = 512
GLA_LOWRANK = 16
GLA_TAU = 16.0
GRID_W = 64
SPLITS = [QK_W, QK_W, GLA_WIDTH, GLA_WIDTH, GLA_LOWRANK, GLA_LOWRANK, CONV_WIDTH, CONV_WIDTH, CONV_WIDTH]

LANES = 128
ROW_GROUP = 16
CHUNK = 128
N_LEVELS = 7
N_HALF_LEVELS = 3
TILE_FFN = 512
FF_CHUNKS = ((0, 1024), (1024, 2048), (2048, D_FF))
TILE_MIX = 256
FFN_LEAD = 1
LR_PAD = LANES
QKV_W = 2 * QK_W + GLA_WIDTH
VMEM_LIMIT_FFN = 56 * 1024 * 1024
VMEM_LIMIT_MIX = 48 * 1024 * 1024
LOG2E = math.log2(math.e)

_NT = (((1,), (1,)), ((), ()))


def _dot(a, b):
    return jnp.dot(a, b, preferred_element_type=F32)


def _dot_nt(a, b):
    return lax.dot_general(a, b, _NT, preferred_element_type=F32)


def _silu(x):
    return x / (1.0 + jnp.exp(-x))


def _rms(x):
    return x * lax.rsqrt(jnp.mean(x * x, axis=-1, keepdims=True) + EPS)


def _rms_mod(x, gain, mod_ref, base):
    return _rms(x) * gain * (1.0 + mod_ref[base + 1:base + 2, :]) + mod_ref[base:base + 1, :]


def _const_spec(shape):
    nd = len(shape)
    return pl.BlockSpec(shape, lambda *_: (0,) * nd, pipeline_mode=pl.Buffered(1))


def _ada_kernel(c_ref, w_ref, b_ref, o_ref):
    a = _silu(c_ref[...]).astype(BF16)
    o_ref[...] = _dot(a, w_ref[...].astype(BF16)) + b_ref[...]


def _ada_mod(cvec, w, b):
    rows = cvec.shape[0]
    n = w.shape[1]
    tn = D_MODEL
    return pl.pallas_call(
        _ada_kernel,
        grid=(n // tn,),
        in_specs=[
            pl.BlockSpec((rows, D_MODEL), lambda i: (0, 0)),
            pl.BlockSpec((D_MODEL, tn), lambda i: (0, i)),
            pl.BlockSpec((1, tn), lambda i: (0, i)),
        ],
        out_specs=pl.BlockSpec((rows, tn), lambda i: (0, i)),
        out_shape=jax.ShapeDtypeStruct((rows, n), F32),
        name="ada_mod",
    )(cvec, w, b.reshape(1, n))


def _interleave(generators, lead=0):
    active = list(generators)
    for _ in range(lead):
        next(active[0])
    while active:
        for g in list(active):
            try:
                next(g)
            except StopIteration:
                active.remove(g)


def _ffn_core(x, mod_ref, base, norm_ref, w1_ref, w3_ref, w2_ref, out):
    h = _rms_mod(x, norm_ref[...], mod_ref, base).astype(BF16)
    yield
    acc = None
    for c0, c1 in FF_CHUNKS:
        a = _dot(h, w1_ref[:, c0:c1])
        b = _dot(h, w3_ref[:, c0:c1])
        yield
        part = _dot((_silu(a) * b).astype(BF16), w2_ref[c0:c1, :])
        acc = part if acc is None else acc + part
        yield
    out.append(x + (0.5 * mod_ref[base + 2:base + 3, :]) * acc)


def _row_halves(n):
    half = n // 2
    return [slice(0, half), slice(half, n)]


def _ffn_proj_kernel(x_ref, mod_ref, n1_ref, w1_ref, w3_ref, w2_ref, n2_ref, wqkv_ref, wlr_ref, wkt_ref,
                     xo_ref, qkv_ref, lr_ref, kt_ref):
    def half(rows):
        out = []
        yield from _ffn_core(x_ref[rows, :], mod_ref, 0, n1_ref, w1_ref, w3_ref, w2_ref, out)
        y = out[0]
        xo_ref[rows, :] = y
        h2 = _rms_mod(y, n2_ref[...], mod_ref, 3).astype(BF16)
        yield
        qkv_ref[rows, :] = _dot(h2, wqkv_ref[...]).astype(BF16)
        yield
        lr_ref[rows, :] = _dot(h2, wlr_ref[...]).astype(BF16)
        kt_ref[:, rows] = _dot_nt(wkt_ref[...], h2).astype(BF16)

    _interleave([half(rows) for rows in _row_halves(x_ref.shape[0])], lead=FFN_LEAD)


def _mod_spec(mod_base, rows_per_mod, tile):
    per = rows_per_mod // tile
    return pl.BlockSpec((None, N_MOD, D_MODEL), lambda i: (mod_base + i // per, 0, 0))


def _ffn_weight_specs():
    return [_const_spec((1, D_MODEL)), _const_spec((D_MODEL, D_FF)), _const_spec((D_MODEL, D_FF)),
            _const_spec((D_FF, D_MODEL))]


def _ffn_proj(x, mods, mod_base, rows_per_mod, lw):
    n = x.shape[0]
    tm = TILE_FFN
    row = lambda i: (i, 0)
    return pl.pallas_call(
        _ffn_proj_kernel,
        grid=(n // tm,),
        in_specs=[pl.BlockSpec((tm, D_MODEL), row), _mod_spec(mod_base, rows_per_mod, tm)]
        + _ffn_weight_specs()
        + [_const_spec((1, D_MODEL)), _const_spec((D_MODEL, QKV_W)), _const_spec((D_MODEL, LR_PAD)),
           _const_spec((QK_W, D_MODEL))],
        out_specs=[
            pl.BlockSpec((tm, D_MODEL), row),
            pl.BlockSpec((tm, QKV_W), row),
            pl.BlockSpec((tm, LR_PAD), row),
            pl.BlockSpec((QK_W, tm), lambda i: (0, i)),
        ],
        out_shape=[
            jax.ShapeDtypeStruct((n, D_MODEL), F32),
            jax.ShapeDtypeStruct((n, QKV_W), BF16),
            jax.ShapeDtypeStruct((n, LR_PAD), BF16),
            jax.ShapeDtypeStruct((QK_W, n), BF16),
        ],
        compiler_params=pltpu.CompilerParams(
            dimension_semantics=("arbitrary",), vmem_limit_bytes=VMEM_LIMIT_FFN),
        name="ffn1_proj",
    )(x, mods, lw["n1"], lw["w1a"], lw["w3a"], lw["w2a"], lw["n_mix"], lw["w_qkv"], lw["w_lr"], lw["wkt"])


def _gla_tables():
    c = CHUNK
    t = np.arange(c)[:, None]
    u = np.arange(c)[None, :]
    dsum = np.zeros((2, N_LEVELS + 1, c, c), np.float32)
    mask = np.zeros((2, N_LEVELS + 1, c, c), np.float32)
    drest = np.zeros((2, c, c), np.float32)
    for lvl in range(N_LEVELS):
        m = c >> (lvl + 1)
        mid = (t // (2 * m)) * (2 * m) + m
        mid_u = (u // (2 * m)) * (2 * m) + m
        same = (t // (2 * m)) == (u // (2 * m))
        hi_t, hi_u = t >= mid, u >= mid_u
        dsum[0, lvl] = np.where(hi_t, (u >= mid) & (u <= t), (u > t) & (u < mid))
        mask[0, lvl] = same & hi_t & ~hi_u
        dsum[1, lvl] = np.where(hi_t, (u >= mid) & (u < t), (u >= t) & (u < mid))
        mask[1, lvl] = same & ~hi_t & hi_u
    dsum[0, N_LEVELS] = u <= t
    dsum[1, N_LEVELS] = u >= t
    mask[:, N_LEVELS] = t == u
    drest[0] = u > t
    drest[1] = u < t
    dsum = dsum.reshape(2, (N_LEVELS + 1) * c, c)
    dsum2 = np.concatenate([dsum, dsum], axis=2)
    rest_t = np.transpose(drest, (0, 2, 1))
    rest1 = np.concatenate([rest_t, np.ones_like(rest_t)], axis=2)
    rest2 = np.concatenate([rest1, rest1], axis=1)
    mask2 = np.concatenate([mask, mask], axis=2)
    return (jnp.asarray(dsum2, BF16), jnp.asarray(rest2, BF16), jnp.asarray(mask2, F32))


def _query_rows(lvl, d):
    if lvl >= N_HALF_LEVELS:
        return [(0, 2 * CHUNK)]
    m = CHUNK >> (lvl + 1)
    assert m % ROW_GROUP == 0
    lo = m if d == 0 else 0
    return [(hh * CHUNK + b0 + lo, hh * CHUNK + b0 + lo + m)
            for hh in range(2) for b0 in range(0, CHUNK, 2 * m)]


def _split_hi_lo(x):
    hi = x.astype(BF16)
    lo = (x - hi.astype(F32)).astype(BF16)
    return hi, lo


def _log2_decay(x):
    log_sigmoid = jnp.minimum(x, 0.0) - jnp.log(1.0 + jnp.exp(-jnp.abs(x)))
    return log_sigmoid * (LOG2E / GLA_TAU)


def _rows(x, ranges):
    parts = [x[r0:r1] for r0, r1 in ranges]
    return parts[0] if len(parts) == 1 else jnp.concatenate(parts, axis=0)


def _gla_chunk(d, q, k, v, kt, la_hl, lat_hl, dsum_ref, rest_ref, mask_ref, state, outs):
    e_all = jnp.exp2(_dot(dsum_ref[...], la_hl)).astype(BF16)
    yield
    r = _dot(lat_hl, rest_ref[...])
    k_rest_t = kt * jnp.exp2(r[:, :CHUNK]).astype(BF16)
    gamma = jnp.exp2(r[:, CHUNK:])
    lane = lax.broadcasted_iota(jnp.int32, (CHUNK, LANES), 1)
    zero = jnp.zeros((), BF16)
    upd = []
    for pr in range(GLA_HEADS // 2):
        ls = slice(LANES * pr, LANES * (pr + 1))
        qp, kp = q[:, ls], k[:, ls]
        qm = jnp.concatenate([jnp.where(lane < GLA_DK, qp, zero),
                              jnp.where(lane >= GLA_DK, qp, zero)], axis=0)
        diag = _dot_nt(qm, kp) * mask_ref[N_LEVELS]
        groups = [diag[g0:g0 + ROW_GROUP] for g0 in range(0, 2 * CHUNK, ROW_GROUP)]
        yield
        for lvl in range(N_LEVELS):
            e = e_all[CHUNK * lvl:CHUNK * (lvl + 1), ls]
            ranges = _query_rows(lvl, d)
            lhs = _rows(qm, ranges) * _rows(jnp.concatenate([e, e], axis=0), ranges)
            part = _dot_nt(lhs, kp * e)
            off = 0
            for r0, r1 in ranges:
                for g0 in range(r0, r1, ROW_GROUP):
                    groups[g0 // ROW_GROUP] += part[off:off + ROW_GROUP] * mask_ref[lvl, g0:g0 + ROW_GROUP, :]
                    off += ROW_GROUP
            yield
        scores = jnp.concatenate(groups, axis=0).astype(BF16)
        e = e_all[CHUNK * N_LEVELS:, ls]
        q_in = qm * jnp.concatenate([e, e], axis=0)
        o_inter = _dot(q_in, state[0][LANES * pr:LANES * (pr + 1), :].astype(BF16))
        for hh in range(2):
            h = 2 * pr + hh
            vh = v[:, GLA_DV * h:GLA_DV * (h + 1)]
            rows = slice(CHUNK * hh, CHUNK * (hh + 1))
            outs.append(o_inter[rows] + _dot(scores[rows], vh))
            upd.append(_dot(k_rest_t[GLA_DK * h:GLA_DK * (h + 1), :], vh))
        yield
    state[0] = gamma * state[0] + jnp.concatenate(upd, axis=0)


def _gla_kernel(*refs, nt, has_s0):
    ins, refs = refs[:8], refs[8:]
    if has_s0:
        s0_ref, refs = refs[0], refs[1:]
    (wd_ref, bd_ref, wdt_ref, bdt_ref, dsum_ref, rest_ref, mask_ref,
     of_ref, ob_ref, st_ref, s_ref) = refs
    n_chunks = TILE_MIX // CHUNK

    @pl.when(pl.program_id(1) == 0)
    def _():
        s_ref[...] = s0_ref[...] if has_s0 else jnp.zeros(s_ref.shape, F32)

    def scan(d):
        qk_ref, v_ref, lr_ref, kt_ref = ins[4 * d:4 * d + 4]
        o_ref = (of_ref, ob_ref)[d]
        lr = lr_ref[...]
        la_hi, la_lo = _split_hi_lo(_log2_decay(_dot(lr, wd_ref[d]) + bd_ref[d]))
        lat_hi, lat_lo = _split_hi_lo(_log2_decay(_dot_nt(wdt_ref[d], lr) + bdt_ref[d]))
        q_all = qk_ref[:, :QK_W] * jnp.asarray(GLA_DK ** -0.5, BF16)
        k_all = qk_ref[:, QK_W:]
        kt_all = kt_ref[...]
        state = [s_ref[d]]
        yield
        for c in (range(n_chunks) if d == 0 else reversed(range(n_chunks))):
            rows = slice(CHUNK * c, CHUNK * (c + 1))
            la_hl = jnp.concatenate([la_hi[rows], la_lo[rows]], axis=0)
            lat_hl = jnp.concatenate([lat_hi[:, rows], lat_lo[:, rows]], axis=1)
            outs = []
            yield from _gla_chunk(d, q_all[rows], k_all[rows], v_ref[rows, :], kt_all[:, rows], la_hl, lat_hl,
                                  dsum_ref.at[d], rest_ref.at[d], mask_ref.at[d], state, outs)
            for h in range(GLA_HEADS):
                o_ref[rows, GLA_DV * h:GLA_DV * (h + 1)] = outs[h]
        s_ref[d] = state[0]
        st_ref[d] = state[0]

    _interleave([scan(0), scan(1)])


def _gla(qkv, lr, kt, s0, seq_len, lw):
    n = qkv.shape[0]
    batch = n // seq_len
    t = TILE_MIX
    nt = seq_len // t
    tile = (lambda b, j: b * nt + j, lambda b, j: b * nt + nt - 1 - j)
    in_specs, args = [], []
    for d in range(2):
        in_specs += [
            pl.BlockSpec((t, 2 * QK_W), lambda b, j, d=d: (tile[d](b, j), 0)),
            pl.BlockSpec((t, GLA_WIDTH), lambda b, j, d=d: (tile[d](b, j), 1)),
            pl.BlockSpec((t, LR_PAD), lambda b, j, d=d: (tile[d](b, j), 0)),
            pl.BlockSpec((QK_W, t), lambda b, j, d=d: (0, tile[d](b, j))),
        ]
        args += [qkv, qkv, lr, kt]
    state_spec = pl.BlockSpec((None, 2, QK_W, GLA_DV), lambda b, j: (b, 0, 0, 0))
    if s0 is not None:
        in_specs.append(state_spec)
        args.append(s0)
    tables = [lw["wd"], lw["bd"], lw["wdt"], lw["bdt"]] + list(lw["tables"])
    in_specs += [_const_spec(a.shape) for a in tables]
    args += tables
    return pl.pallas_call(
        functools.partial(_gla_kernel, nt=nt, has_s0=s0 is not None),
        grid=(batch, nt),
        in_specs=in_specs,
        out_specs=[
            pl.BlockSpec((t, GLA_WIDTH), lambda b, j: (tile[0](b, j), 0)),
            pl.BlockSpec((t, GLA_WIDTH), lambda b, j: (tile[1](b, j), 0)),
            state_spec,
        ],
        out_shape=[
            jax.ShapeDtypeStruct((n, GLA_WIDTH), F32),
            jax.ShapeDtypeStruct((n, GLA_WIDTH), F32),
            jax.ShapeDtypeStruct((batch, 2, QK_W, GLA_DV), F32),
        ],
        scratch_shapes=[pltpu.VMEM((2, QK_W, GLA_DV), F32)],
        compiler_params=pltpu.CompilerParams(
            dimension_semantics=("arbitrary", "arbitrary"), vmem_limit_bytes=VMEM_LIMIT_MIX),
        name="gla",
    )(*args)


def _mix_ffn_kernel(x_ref, of_ref, ob_ref, mod_ref, nm_ref, wg_ref, wc_ref, gn_ref, cw_ref, wout_ref,
                    n2_ref, w1_ref, w3_ref, w2_ref, fn_ref, yo_ref, *, period):
    def half(rows):
        x = x_ref[rows, :]
        h2 = _rms_mod(x, nm_ref[...], mod_ref, 3).astype(BF16)
        o = of_ref[rows, :] + ob_ref[rows, :]
        yield
        g = _dot(h2, wg_ref[...])
        parts = []
        for h in range(GLA_HEADS):
            cols = slice(GLA_DV * h, GLA_DV * (h + 1))
            parts.append(_rms(o[:, cols]) * gn_ref[:, cols] * _silu(g[:, cols]))
        yield
        conv = _dot(h2, wc_ref[...])
        cb = conv[:, :CONV_WIDTH]
        u = conv[:, CONV_WIDTH:2 * CONV_WIDTH] * conv[:, 2 * CONV_WIDTH:]
        pos = lax.broadcasted_iota(jnp.int32, u.shape, 0) & (period - 1)
        u_prev = jnp.where(pos == 0, 0.0, pltpu.roll(u, 1, 0))
        u_next = jnp.where(pos == period - 1, 0.0, pltpu.roll(u, u.shape[0] - 1, 0))
        cu = cw_ref[0:1, :] * u_prev + cw_ref[1:2, :] * u + cw_ref[2:3, :] * u_next
        parts.append(cb * cu)
        mixed = jnp.concatenate(parts, axis=-1).astype(BF16)
        yield
        x2 = x + mod_ref[5:6, :] * _dot(mixed, wout_ref[...])
        out = []
        yield from _ffn_core(x2, mod_ref, 6, n2_ref, w1_ref, w3_ref, w2_ref, out)
        yo_ref[rows, :] = _rms(out[0]) * fn_ref[...]

    _interleave([half(rows) for rows in _row_halves(x_ref.shape[0])], lead=FFN_LEAD)


def _mix_ffn(x, o_fwd, o_bwd, mods, mod_base, rows_per_mod, period, lw, fn):
    n = x.shape[0]
    tm = TILE_FFN
    assert period & (period - 1) == 0 and (tm // 2) % period == 0
    row = lambda i: (i, 0)
    return pl.pallas_call(
        functools.partial(_mix_ffn_kernel, period=period),
        grid=(n // tm,),
        in_specs=[pl.BlockSpec((tm, D_MODEL), row), pl.BlockSpec((tm, GLA_WIDTH), row),
                  pl.BlockSpec((tm, GLA_WIDTH), row), _mod_spec(mod_base, rows_per_mod, tm),
                  _const_spec((1, D_MODEL)), _const_spec((D_MODEL, GLA_WIDTH)),
                  _const_spec((D_MODEL, 3 * CONV_WIDTH)), _const_spec((1, GLA_WIDTH)),
                  _const_spec((3, CONV_WIDTH)), _const_spec((D_MODEL, D_MODEL))]
        + _ffn_weight_specs() + [_const_spec((1, D_MODEL))],
        out_specs=pl.BlockSpec((tm, D_MODEL), row),
        out_shape=jax.ShapeDtypeStruct((n, D_MODEL), F32),
        compiler_params=pltpu.CompilerParams(
            dimension_semantics=("arbitrary",), vmem_limit_bytes=VMEM_LIMIT_FFN),
        name="mix_ffn2",
    )(x, o_fwd, o_bwd, mods, lw["n_mix"], lw["w_g"], lw["w_conv"], lw["gla_norm"], lw["conv_w"], lw["w_out"],
      lw["n2"], lw["w1b"], lw["w3b"], lw["w2b"], fn)


def _split_w_in_kernel(w_ref, qkv_ref, g_ref, conv_ref, lr_ref, kt_ref):
    edges = [0] + list(np.cumsum(SPLITS))
    w = w_ref[...]
    qkv_ref[...] = w[:, :edges[3]].astype(BF16)
    g_ref[...] = w[:, edges[3]:edges[4]].astype(BF16)
    conv_ref[...] = w[:, edges[6]:].astype(BF16)
    lane = lax.broadcasted_iota(jnp.int32, (w.shape[0], LR_PAD), 1)
    lr_ref[...] = jnp.where(lane < 2 * GLA_LOWRANK, w[:, edges[4]:edges[4] + LR_PAD], 0.0).astype(BF16)
    kt_ref[...] = w[:, edges[1]:edges[2]].T.astype(BF16)


def _split_w_in(w_in, l):
    rows = QK_W
    n_cols = w_in.shape[2]
    row = lambda i: (i, 0)
    return pl.pallas_call(
        _split_w_in_kernel,
        grid=(D_MODEL // rows,),
        in_specs=[pl.BlockSpec((None, rows, n_cols), lambda i: (l, i, 0))],
        out_specs=[pl.BlockSpec((rows, QKV_W), row), pl.BlockSpec((rows, GLA_WIDTH), row),
                   pl.BlockSpec((rows, 3 * CONV_WIDTH), row), pl.BlockSpec((rows, LR_PAD), row),
                   pl.BlockSpec((QK_W, rows), lambda i: (0, i))],
        out_shape=[jax.ShapeDtypeStruct((D_MODEL, QKV_W), BF16), jax.ShapeDtypeStruct((D_MODEL, GLA_WIDTH), BF16),
                   jax.ShapeDtypeStruct((D_MODEL, 3 * CONV_WIDTH), BF16),
                   jax.ShapeDtypeStruct((D_MODEL, LR_PAD), BF16), jax.ShapeDtypeStruct((QK_W, D_MODEL), BF16)],
        name="split_w_in",
    )(w_in)


def _prep_layer(l, norm_ffn1, w1_ffn1, w3_ffn1, w2_ffn1, norm_mix, w_in, w_decay, b_decay, gla_norm,
                conv_w, w_out, norm_ffn2, w1_ffn2, w3_ffn2, w2_ffn2, tables):
    w_qkv, w_g, w_conv, w_lr, wkt = _split_w_in(w_in, l)
    wd = jnp.zeros((2, LR_PAD, QK_W), F32)
    wd = wd.at[0, :GLA_LOWRANK].set(w_decay[l, 0]).at[1, GLA_LOWRANK:2 * GLA_LOWRANK].set(w_decay[l, 1])
    bd = b_decay[l].reshape(2, 1, QK_W)
    return dict(
        n1=norm_ffn1[l].reshape(1, -1), w1a=w1_ffn1[l].astype(BF16), w3a=w3_ffn1[l].astype(BF16),
        w2a=w2_ffn1[l].astype(BF16),
        n_mix=norm_mix[l].reshape(1, -1), w_qkv=w_qkv, w_lr=w_lr, wkt=wkt, w_g=w_g, w_conv=w_conv,
        wd=wd.astype(BF16), bd=bd, wdt=jnp.transpose(wd, (0, 2, 1)).astype(BF16),
        bdt=jnp.broadcast_to(jnp.transpose(bd, (0, 2, 1)), (2, QK_W, TILE_MIX)),
        gla_norm=gla_norm[l].reshape(1, -1), conv_w=conv_w[l], w_out=w_out[l].astype(BF16),
        n2=norm_ffn2[l].reshape(1, -1), w1b=w1_ffn2[l].astype(BF16), w3b=w3_ffn2[l].astype(BF16),
        w2b=w2_ffn2[l].astype(BF16), tables=tables)


def kernel(x_prompt, x_sample, state_gla, c, c_ctx, w_ada, b_ada, norm_ffn1, w1_ffn1, w3_ffn1, w2_ffn1,
           norm_mix, w_in, w_decay, b_decay, gla_norm, conv_w, w_out, norm_ffn2, w1_ffn2, w3_ffn2,
           w2_ffn2, final_norm):
    batch, seq, _ = x_prompt.shape
    dec_batch, dec_seq, _ = x_sample.shape
    depth = w_ada.shape[0]
    assert depth == 1, "the fused final norm assumes a single layer"
    l = 0
    xp = x_prompt.reshape(batch * seq, D_MODEL)
    xs = x_sample.reshape(dec_batch * dec_seq, D_MODEL)
    mod_rows = 8
    assert 1 + dec_batch <= mod_rows
    cvec = jnp.concatenate([c_ctx[None, :], c, jnp.zeros((mod_rows - 1 - dec_batch, D_MODEL), F32)], axis=0)
    fn = final_norm.reshape(1, -1)
    lw = _prep_layer(l, norm_ffn1, w1_ffn1, w3_ffn1, w2_ffn1, norm_mix, w_in, w_decay, b_decay,
                     gla_norm, conv_w, w_out, norm_ffn2, w1_ffn2, w3_ffn2, w2_ffn2, _gla_tables())
    mods = _ada_mod(cvec, w_ada[l], b_ada[l]).reshape(mod_rows, N_MOD, D_MODEL)
    s0 = state_gla[:, l].reshape(dec_batch, 2, QK_W, GLA_DV)
    results = []
    for (x, mod_base, rows_per_mod, init, slen, period) in (
            (xp, 0, batch * seq, None, seq, seq),
            (xs, 1, dec_seq, s0, dec_seq, GRID_W)):
        x1, qkv, lr, kt = _ffn_proj(x, mods, mod_base, rows_per_mod, lw)
        o_fwd, o_bwd, st = _gla(qkv, lr, kt, init, slen, lw)
        results.append((_mix_ffn(x1, o_fwd, o_bwd, mods, mod_base, rows_per_mod, period, lw, fn), st))
    (yp, st_p), (ys, _) = results
    y_prompt = yp.reshape(batch, seq, D_MODEL)
    y_sample = ys.reshape(dec_batch, dec_seq, D_MODEL)
    new_state = st_p.reshape(batch, 1, 2, GLA_HEADS, GLA_DK, GLA_DV)
    return (y_prompt, y_sample, new_state)
```

```python
import functools
import math

import numpy as np
import jax
import jax.numpy as jnp
from jax import lax
from jax.experimental import pallas as pl
from jax.experimental.pallas import tpu as pltpu

F32 = jnp.float32
BF16 = jnp.bfloat16

D_MODEL = 1024
D_FF = 2816
N_MOD = 9
EPS = 1e-6
GLA_HEADS = 4
GLA_DK = 64
GLA_DV = 128
QK_W = GLA_HEADS * GLA_DK
GLA_WIDTH = GLA_HEADS * GLA_DV
CONV_WIDTH = 512
GLA_LOWRANK = 16
GLA_TAU = 16.0
GRID_W = 64
SPLITS = [QK_W, QK_W, GLA_WIDTH, GLA_WIDTH, GLA_LOWRANK, GLA_LOWRANK, CONV_WIDTH, CONV_WIDTH, CONV_WIDTH]

LANES = 128
ROW_GROUP = 16
CHUNK = 128
N_LEVELS = 7
N_HALF_LEVELS = 3
TILE_FFN = 512
FF_CHUNKS = ((0, 1024), (1024, 2048), (2048, D_FF))
TILE_MIX = 256
FFN_LEAD = 1
LR_PAD = LANES
QKV_W = 2 * QK_W + GLA_WIDTH
VMEM_LIMIT_FFN = 56 * 1024 * 1024
VMEM_LIMIT_MIX = 48 * 1024 * 1024
LOG2E = math.log2(math.e)

_NT = (((1,), (1,)), ((), ()))


def _dot(a, b):
    return jnp.dot(a, b, preferred_element_type=F32)


def _dot_nt(a, b):
    return lax.dot_general(a, b, _NT, preferred_element_type=F32)


def _silu(x):
    return x / (1.0 + jnp.exp(-x))


def _rms(x):
    return x * lax.rsqrt(jnp.mean(x * x, axis=-1, keepdims=True) + EPS)


def _rms_mod(x, gain, mod_ref, base):
    return _rms(x) * gain * (1.0 + mod_ref[base + 1:base + 2, :]) + mod_ref[base:base + 1, :]


def _const_spec(shape):
    nd = len(shape)
    return pl.BlockSpec(shape, lambda *_: (0,) * nd, pipeline_mode=pl.Buffered(1))


def _ada_kernel(c_ref, w_ref, b_ref, o_ref):
    a = _silu(c_ref[...]).astype(BF16)
    o_ref[...] = _dot(a, w_ref[...].astype(BF16)) + b_ref[...]


def _ada_mod(cvec, w, b):
    rows = cvec.shape[0]
    n = w.shape[1]
    tn = D_MODEL
    return pl.pallas_call(
        _ada_kernel,
        grid=(n // tn,),
        in_specs=[
            pl.BlockSpec((rows, D_MODEL), lambda i: (0, 0)),
            pl.BlockSpec((D_MODEL, tn), lambda i: (0, i)),
            pl.BlockSpec((1, tn), lambda i: (0, i)),
        ],
        out_specs=pl.BlockSpec((rows, tn), lambda i: (0, i)),
        out_shape=jax.ShapeDtypeStruct((rows, n), F32),
        name="ada_mod",
    )(cvec, w, b.reshape(1, n))


def _interleave(generators, lead=0):
    active = list(generators)
    for _ in range(lead):
        next(active[0])
    while active:
        for g in list(active):
            try:
                next(g)
            except StopIteration:
                active.remove(g)


def _ffn_core(x, mod_ref, base, norm_ref, w1_ref, w3_ref, w2_ref, out):
    h = _rms_mod(x, norm_ref[...], mod_ref, base).astype(BF16)
    yield
    acc = None
    for c0, c1 in FF_CHUNKS:
        a = _dot(h, w1_ref[:, c0:c1])
        b = _dot(h, w3_ref[:, c0:c1])
        yield
        part = _dot((_silu(a) * b).astype(BF16), w2_ref[c0:c1, :])
        acc = part if acc is None else acc + part
        yield
    out.append(x + (0.5 * mod_ref[base + 2:base + 3, :]) * acc)


def _row_halves(n):
    half = n // 2
    return [slice(0, half), slice(half, n)]


def _ffn_proj_kernel(x_ref, mod_ref, n1_ref, w1_ref, w3_ref, w2_ref, n2_ref, wqkv_ref, wlr_ref, wkt_ref,
                     xo_ref, qkv_ref, lr_ref, kt_ref):
    def half(rows):
        out = []
        yield from _ffn_core(x_ref[rows, :], mod_ref, 0, n1_ref, w1_ref, w3_ref, w2_ref, out)
        y = out[0]
        xo_ref[rows, :] = y
        h2 = _rms_mod(y, n2_ref[...], mod_ref, 3).astype(BF16)
        yield
        qkv_ref[rows, :] = _dot(h2, wqkv_ref[...]).astype(BF16)
        yield
        lr_ref[rows, :] = _dot(h2, wlr_ref[...]).astype(BF16)
        kt_ref[:, rows] = _dot_nt(wkt_ref[...], h2).astype(BF16)

    _interleave([half(rows) for rows in _row_halves(x_ref.shape[0])], lead=FFN_LEAD)


def _mod_spec(mod_base, rows_per_mod, tile):
    per = rows_per_mod // tile
    return pl.BlockSpec((None, N_MOD, D_MODEL), lambda i: (mod_base + i // per, 0, 0))


def _ffn_weight_specs():
    return [_const_spec((1, D_MODEL)), _const_spec((D_MODEL, D_FF)), _const_spec((D_MODEL, D_FF)),
            _const_spec((D_FF, D_MODEL))]


def _ffn_proj(x, mods, mod_base, rows_per_mod, lw):
    n = x.shape[0]
    tm = TILE_FFN
    row = lambda i: (i, 0)
    return pl.pallas_call(
        _ffn_proj_kernel,
        grid=(n // tm,),
        in_specs=[pl.BlockSpec((tm, D_MODEL), row), _mod_spec(mod_base, rows_per_mod, tm)]
        + _ffn_weight_specs()
        + [_const_spec((1, D_MODEL)), _const_spec((D_MODEL, QKV_W)), _const_spec((D_MODEL, LR_PAD)),
           _const_spec((QK_W, D_MODEL))],
        out_specs=[
            pl.BlockSpec((tm, D_MODEL), row),
            pl.BlockSpec((tm, QKV_W), row),
            pl.BlockSpec((tm, LR_PAD), row),
            pl.BlockSpec((QK_W, tm), lambda i: (0, i)),
        ],
        out_shape=[
            jax.ShapeDtypeStruct((n, D_MODEL), F32),
            jax.ShapeDtypeStruct((n, QKV_W), BF16),
            jax.ShapeDtypeStruct((n, LR_PAD), BF16),
            jax.ShapeDtypeStruct((QK_W, n), BF16),
        ],
        compiler_params=pltpu.CompilerParams(
            dimension_semantics=("arbitrary",), vmem_limit_bytes=VMEM_LIMIT_FFN),
        name="ffn1_proj",
    )(x, mods, lw["n1"], lw["w1a"], lw["w3a"], lw["w2a"], lw["n_mix"], lw["w_qkv"], lw["w_lr"], lw["wkt"])


def _gla_tables():
    c = CHUNK
    t = np.arange(c)[:, None]
    u = np.arange(c)[None, :]
    dsum = np.zeros((2, N_LEVELS + 1, c, c), np.float32)
    mask = np.zeros((2, N_LEVELS + 1, c, c), np.float32)
    drest = np.zeros((2, c, c), np.float32)
    for lvl in range(N_LEVELS):
        m = c >> (lvl + 1)
        mid = (t // (2 * m)) * (2 * m) + m
        mid_u = (u // (2 * m)) * (2 * m) + m
        same = (t // (2 * m)) == (u // (2 * m))
        hi_t, hi_u = t >= mid, u >= mid_u
        dsum[0, lvl] = np.where(hi_t, (u >= mid) & (u <= t), (u > t) & (u < mid))
        mask[0, lvl] = same & hi_t & ~hi_u
        dsum[1, lvl] = np.where(hi_t, (u >= mid) & (u < t), (u >= t) & (u < mid))
        mask[1, lvl] = same & ~hi_t & hi_u
    dsum[0, N_LEVELS] = u <= t
    dsum[1, N_LEVELS] = u >= t
    mask[:, N_LEVELS] = t == u
    drest[0] = u > t
    drest[1] = u < t
    dsum = dsum.reshape(2, (N_LEVELS + 1) * c, c)
    dsum2 = np.concatenate([dsum, dsum], axis=2)
    rest_t = np.transpose(drest, (0, 2, 1))
    rest1 = np.concatenate([rest_t, np.ones_like(rest_t)], axis=2)
    rest2 = np.concatenate([rest1, rest1], axis=1)
    mask2 = np.concatenate([mask, mask], axis=2)
    return (jnp.asarray(dsum2, BF16), jnp.asarray(rest2, BF16), jnp.asarray(mask2, F32))


def _query_rows(lvl, d):
    if lvl >= N_HALF_LEVELS:
        return [(0, 2 * CHUNK)]
    m = CHUNK >> (lvl + 1)
    assert m % ROW_GROUP == 0
    lo = m if d == 0 else 0
    return [(hh * CHUNK + b0 + lo, hh * CHUNK + b0 + lo + m)
            for hh in range(2) for b0 in range(0, CHUNK, 2 * m)]


def _split_hi_lo(x):
    hi = x.astype(BF16)
    lo = (x - hi.astype(F32)).astype(BF16)
    return hi, lo


def _log2_decay(x):
    log_sigmoid = jnp.minimum(x, 0.0) - jnp.log(1.0 + jnp.exp(-jnp.abs(x)))
    return log_sigmoid * (LOG2E / GLA_TAU)


def _rows(x, ranges):
    parts = [x[r0:r1] for r0, r1 in ranges]
    return parts[0] if len(parts) == 1 else jnp.concatenate(parts, axis=0)


def _gla_chunk(d, q, k, v, kt, la_hl, lat_hl, dsum_ref, rest_ref, mask_ref, state, outs):
    e_all = jnp.exp2(_dot(dsum_ref[...], la_hl)).astype(BF16)
    yield
    r = _dot(lat_hl, rest_ref[...])
    k_rest_t = kt * jnp.exp2(r[:, :CHUNK]).astype(BF16)
    gamma = jnp.exp2(r[:, CHUNK:])
    lane = lax.broadcasted_iota(jnp.int32, (CHUNK, LANES), 1)
    zero = jnp.zeros((), BF16)
    upd = []
    for pr in range(GLA_HEADS // 2):
        ls = slice(LANES * pr, LANES * (pr + 1))
        qp, kp = q[:, ls], k[:, ls]
        qm = jnp.concatenate([jnp.where(lane < GLA_DK, qp, zero),
                              jnp.where(lane >= GLA_DK, qp, zero)], axis=0)
        diag = _dot_nt(qm, kp) * mask_ref[N_LEVELS]
        groups = [diag[g0:g0 + ROW_GROUP] for g0 in range(0, 2 * CHUNK, ROW_GROUP)]
        yield
        for lvl in range(N_LEVELS):
            e = e_all[CHUNK * lvl:CHUNK * (lvl + 1), ls]
            ranges = _query_rows(lvl, d)
            lhs = _rows(qm, ranges) * _rows(jnp.concatenate([e, e], axis=0), ranges)
            part = _dot_nt(lhs, kp * e)
            off = 0
            for r0, r1 in ranges:
                for g0 in range(r0, r1, ROW_GROUP):
                    groups[g0 // ROW_GROUP] += part[off:off + ROW_GROUP] * mask_ref[lvl, g0:g0 + ROW_GROUP, :]
                    off += ROW_GROUP
            yield
        scores = jnp.concatenate(groups, axis=0).astype(BF16)
        e = e_all[CHUNK * N_LEVELS:, ls]
        q_in = qm * jnp.concatenate([e, e], axis=0)
        o_inter = _dot(q_in, state[0][LANES * pr:LANES * (pr + 1), :].astype(BF16))
        for hh in range(2):
            h = 2 * pr + hh
            vh = v[:, GLA_DV * h:GLA_DV * (h + 1)]
            rows = slice(CHUNK * hh, CHUNK * (hh + 1))
            outs.append(o_inter[rows] + _dot(scores[rows], vh))
            upd.append(_dot(k_rest_t[GLA_DK * h:GLA_DK * (h + 1), :], vh))
        yield
    state[0] = gamma * state[0] + jnp.concatenate(upd, axis=0)


def _gla_kernel(*refs, nt, has_s0):
    ins, refs = refs[:8], refs[8:]
    if has_s0:
        s0_ref, refs = refs[0], refs[1:]
    (wd_ref, bd_ref, wdt_ref, bdt_ref, dsum_ref, rest_ref, mask_ref,
     of_ref, ob_ref, st_ref, s_ref) = refs
    n_chunks = TILE_MIX // CHUNK

    @pl.when(pl.program_id(1) == 0)
    def _():
        s_ref[...] = s0_ref[...] if has_s0 else jnp.zeros(s_ref.shape, F32)

    def scan(d):
        qk_ref, v_ref, lr_ref, kt_ref = ins[4 * d:4 * d + 4]
        o_ref = (of_ref, ob_ref)[d]
        lr = lr_ref[...]
        la_hi, la_lo = _split_hi_lo(_log2_decay(_dot(lr, wd_ref[d]) + bd_ref[d]))
        lat_hi, lat_lo = _split_hi_lo(_log2_decay(_dot_nt(wdt_ref[d], lr) + bdt_ref[d]))
        q_all = qk_ref[:, :QK_W] * jnp.asarray(GLA_DK ** -0.5, BF16)
        k_all = qk_ref[:, QK_W:]
        kt_all = kt_ref[...]
        state = [s_ref[d]]
        yield
        for c in (range(n_chunks) if d == 0 else reversed(range(n_chunks))):
            rows = slice(CHUNK * c, CHUNK * (c + 1))
            la_hl = jnp.concatenate([la_hi[rows], la_lo[rows]], axis=0)
            lat_hl = jnp.concatenate([lat_hi[:, rows], lat_lo[:, rows]], axis=1)
            outs = []
            yield from _gla_chunk(d, q_all[rows], k_all[rows], v_ref[rows, :], kt_all[:, rows], la_hl, lat_hl,
                                  dsum_ref.at[d], rest_ref.at[d], mask_ref.at[d], state, outs)
            for h in range(GLA_HEADS):
                o_ref[rows, GLA_DV * h:GLA_DV * (h + 1)] = outs[h]
        s_ref[d] = state[0]
        st_ref[d] = state[0]

    _interleave([scan(0), scan(1)])


def _gla(qkv, lr, kt, s0, seq_len, lw):
    n = qkv.shape[0]
    batch = n // seq_len
    t = TILE_MIX
    nt = seq_len // t
    tile = (lambda b, j: b * nt + j, lambda b, j: b * nt + nt - 1 - j)
    in_specs, args = [], []
    for d in range(2):
        in_specs += [
            pl.BlockSpec((t, 2 * QK_W), lambda b, j, d=d: (tile[d](b, j), 0)),
            pl.BlockSpec((t, GLA_WIDTH), lambda b, j, d=d: (tile[d](b, j), 1)),
            pl.BlockSpec((t, LR_PAD), lambda b, j, d=d: (tile[d](b, j), 0)),
            pl.BlockSpec((QK_W, t), lambda b, j, d=d: (0, tile[d](b, j))),
        ]
        args += [qkv, qkv, lr, kt]
    state_spec = pl.BlockSpec((None, 2, QK_W, GLA_DV), lambda b, j: (b, 0, 0, 0))
    if s0 is not None:
        in_specs.append(state_spec)
        args.append(s0)
    tables = [lw["wd"], lw["bd"], lw["wdt"], lw["bdt"]] + list(lw["tables"])
    in_specs += [_const_spec(a.shape) for a in tables]
    args += tables
    return pl.pallas_call(
        functools.partial(_gla_kernel, nt=nt, has_s0=s0 is not None),
        grid=(batch, nt),
        in_specs=in_specs,
        out_specs=[
            pl.BlockSpec((t, GLA_WIDTH), lambda b, j: (tile[0](b, j), 0)),
            pl.BlockSpec((t, GLA_WIDTH), lambda b, j: (tile[1](b, j), 0)),
            state_spec,
        ],
        out_shape=[
            jax.ShapeDtypeStruct((n, GLA_WIDTH), F32),
            jax.ShapeDtypeStruct((n, GLA_WIDTH), F32),
            jax.ShapeDtypeStruct((batch, 2, QK_W, GLA_DV), F32),
        ],
        scratch_shapes=[pltpu.VMEM((2, QK_W, GLA_DV), F32)],
        compiler_params=pltpu.CompilerParams(
            dimension_semantics=("arbitrary", "arbitrary"), vmem_limit_bytes=VMEM_LIMIT_MIX),
        name="gla",
    )(*args)


def _mix_ffn_kernel(x_ref, of_ref, ob_ref, mod_ref, nm_ref, wg_ref, wc_ref, gn_ref, cw_ref, wout_ref,
                    n2_ref, w1_ref, w3_ref, w2_ref, fn_ref, yo_ref, *, period):
    def half(rows):
        x = x_ref[rows, :]
        h2 = _rms_mod(x, nm_ref[...], mod_ref, 3).astype(BF16)
        o = of_ref[rows, :] + ob_ref[rows, :]
        yield
        g = _dot(h2, wg_ref[...])
        parts = []
        for h in range(GLA_HEADS):
            cols = slice(GLA_DV * h, GLA_DV * (h + 1))
            parts.append(_rms(o[:, cols]) * gn_ref[:, cols] * _silu(g[:, cols]))
        yield
        conv = _dot(h2, wc_ref[...])
        cb = conv[:, :CONV_WIDTH]
        u = conv[:, CONV_WIDTH:2 * CONV_WIDTH] * conv[:, 2 * CONV_WIDTH:]
        pos = lax.broadcasted_iota(jnp.int32, u.shape, 0) & (period - 1)
        u_prev = jnp.where(pos == 0, 0.0, pltpu.roll(u, 1, 0))
        u_next = jnp.where(pos == period - 1, 0.0, pltpu.roll(u, u.shape[0] - 1, 0))
        cu = cw_ref[0:1, :] * u_prev + cw_ref[1:2, :] * u + cw_ref[2:3, :] * u_next
        parts.append(cb * cu)
        mixed = jnp.concatenate(parts, axis=-1).astype(BF16)
        yield
        x2 = x + mod_ref[5:6, :] * _dot(mixed, wout_ref[...])
        out = []
        yield from _ffn_core(x2, mod_ref, 6, n2_ref, w1_ref, w3_ref, w2_ref, out)
        yo_ref[rows, :] = _rms(out[0]) * fn_ref[...]

    _interleave([half(rows) for rows in _row_halves(x_ref.shape[0])], lead=FFN_LEAD)


def _mix_ffn(x, o_fwd, o_bwd, mods, mod_base, rows_per_mod, period, lw, fn):
    n = x.shape[0]
    tm = TILE_FFN
    assert period & (period - 1) == 0 and (tm // 2) % period == 0
    row = lambda i: (i, 0)
    return pl.pallas_call(
        functools.partial(_mix_ffn_kernel, period=period),
        grid=(n // tm,),
        in_specs=[pl.BlockSpec((tm, D_MODEL), row), pl.BlockSpec((tm, GLA_WIDTH), row),
                  pl.BlockSpec((tm, GLA_WIDTH), row), _mod_spec(mod_base, rows_per_mod, tm),
                  _const_spec((1, D_MODEL)), _const_spec((D_MODEL, GLA_WIDTH)),
                  _const_spec((D_MODEL, 3 * CONV_WIDTH)), _const_spec((1, GLA_WIDTH)),
                  _const_spec((3, CONV_WIDTH)), _const_spec((D_MODEL, D_MODEL))]
        + _ffn_weight_specs() + [_const_spec((1, D_MODEL))],
        out_specs=pl.BlockSpec((tm, D_MODEL), row),
        out_shape=jax.ShapeDtypeStruct((n, D_MODEL), F32),
        compiler_params=pltpu.CompilerParams(
            dimension_semantics=("arbitrary",), vmem_limit_bytes=VMEM_LIMIT_FFN),
        name="mix_ffn2",
    )(x, o_fwd, o_bwd, mods, lw["n_mix"], lw["w_g"], lw["w_conv"], lw["gla_norm"], lw["conv_w"], lw["w_out"],
      lw["n2"], lw["w1b"], lw["w3b"], lw["w2b"], fn)


def _split_w_in_kernel(wt_ref, qkv_ref, g_ref, conv_ref, lr_ref, kt_ref):
    edges = [0] + list(np.cumsum(SPLITS))
    piece = lambda a, b: wt_ref[edges[a]:edges[b], :]
    qkv_ref[...] = piece(0, 3).T.astype(BF16)
    g_ref[...] = piece(3, 4).T.astype(BF16)
    conv_ref[...] = piece(6, 9).T.astype(BF16)
    lr_t = jnp.concatenate([piece(4, 6), jnp.zeros((LR_PAD - 2 * GLA_LOWRANK, wt_ref.shape[1]), F32)], axis=0)
    lr_ref[...] = lr_t.T.astype(BF16)
    kt_ref[...] = piece(1, 2).astype(BF16)


def _split_w_in(w_in, l):
    rows = QK_W
    n_cols = w_in.shape[2]
    row = lambda i: (i, 0)
    return pl.pallas_call(
        _split_w_in_kernel,
        grid=(D_MODEL // rows,),
        in_specs=[pl.BlockSpec((n_cols, rows), lambda i: (0, i))],
        out_specs=[pl.BlockSpec((rows, QKV_W), row), pl.BlockSpec((rows, GLA_WIDTH), row),
                   pl.BlockSpec((rows, 3 * CONV_WIDTH), row), pl.BlockSpec((rows, LR_PAD), row),
                   pl.BlockSpec((QK_W, rows), lambda i: (0, i))],
        out_shape=[jax.ShapeDtypeStruct((D_MODEL, QKV_W), BF16), jax.ShapeDtypeStruct((D_MODEL, GLA_WIDTH), BF16),
                   jax.ShapeDtypeStruct((D_MODEL, 3 * CONV_WIDTH), BF16),
                   jax.ShapeDtypeStruct((D_MODEL, LR_PAD), BF16), jax.ShapeDtypeStruct((QK_W, D_MODEL), BF16)],
        name="split_w_in",
    )(jnp.transpose(w_in[l]))


def _prep_layer(l, norm_ffn1, w1_ffn1, w3_ffn1, w2_ffn1, norm_mix, w_in, w_decay, b_decay, gla_norm,
                conv_w, w_out, norm_ffn2, w1_ffn2, w3_ffn2, w2_ffn2, tables):
    w_qkv, w_g, w_conv, w_lr, wkt = _split_w_in(w_in, l)
    wd = jnp.zeros((2, LR_PAD, QK_W), F32)
    wd = wd.at[0, :GLA_LOWRANK].set(w_decay[l, 0]).at[1, GLA_LOWRANK:2 * GLA_LOWRANK].set(w_decay[l, 1])
    bd = b_decay[l].reshape(2, 1, QK_W)
    return dict(
        n1=norm_ffn1[l].reshape(1, -1), w1a=w1_ffn1[l].astype(BF16), w3a=w3_ffn1[l].astype(BF16),
        w2a=w2_ffn1[l].astype(BF16),
        n_mix=norm_mix[l].reshape(1, -1), w_qkv=w_qkv, w_lr=w_lr, wkt=wkt, w_g=w_g, w_conv=w_conv,
        wd=wd.astype(BF16), bd=bd, wdt=jnp.transpose(wd, (0, 2, 1)).astype(BF16),
        bdt=jnp.broadcast_to(jnp.transpose(bd, (0, 2, 1)), (2, QK_W, TILE_MIX)),
        gla_norm=gla_norm[l].reshape(1, -1), conv_w=conv_w[l], w_out=w_out[l].astype(BF16),
        n2=norm_ffn2[l].reshape(1, -1), w1b=w1_ffn2[l].astype(BF16), w3b=w3_ffn2[l].astype(BF16),
        w2b=w2_ffn2[l].astype(BF16), tables=tables)


def kernel(x_prompt, x_sample, state_gla, c, c_ctx, w_ada, b_ada, norm_ffn1, w1_ffn1, w3_ffn1, w2_ffn1,
           norm_mix, w_in, w_decay, b_decay, gla_norm, conv_w, w_out, norm_ffn2, w1_ffn2, w3_ffn2,
           w2_ffn2, final_norm):
    batch, seq, _ = x_prompt.shape
    dec_batch, dec_seq, _ = x_sample.shape
    depth = w_ada.shape[0]
    assert depth == 1, "the fused final norm assumes a single layer"
    l = 0
    xp = x_prompt.reshape(batch * seq, D_MODEL)
    xs = x_sample.reshape(dec_batch * dec_seq, D_MODEL)
    mod_rows = 8
    assert 1 + dec_batch <= mod_rows
    cvec = jnp.concatenate([c_ctx[None, :], c, jnp.zeros((mod_rows - 1 - dec_batch, D_MODEL), F32)], axis=0)
    fn = final_norm.reshape(1, -1)
    lw = _prep_layer(l, norm_ffn1, w1_ffn1, w3_ffn1, w2_ffn1, norm_mix, w_in, w_decay, b_decay,
                     gla_norm, conv_w, w_out, norm_ffn2, w1_ffn2, w3_ffn2, w2_ffn2, _gla_tables())
    mods = _ada_mod(cvec, w_ada[l], b_ada[l]).reshape(mod_rows, N_MOD, D_MODEL)
    s0 = state_gla[:, l].reshape(dec_batch, 2, QK_W, GLA_DV)
    results = []
    for (x, mod_base, rows_per_mod, init, slen, period) in (
            (xp, 0, batch * seq, None, seq, seq),
            (xs, 1, dec_seq, s0, dec_seq, GRID_W)):
        x1, qkv, lr, kt = _ffn_proj(x, mods, mod_base, rows_per_mod, lw)
        o_fwd, o_bwd, st = _gla(qkv, lr, kt, init, slen, lw)
        results.append((_mix_ffn(x1, o_fwd, o_bwd, mods, mod_base, rows_per_mod, period, lw, fn), st))
    (yp, st_p), (ys, _) = results
    y_prompt = yp.reshape(batch, seq, D_MODEL)
    y_sample = ys.reshape(dec_batch, dec_seq, D_MODEL)
    new_state = st_p.reshape(batch, 1, 2, GLA_HEADS, GLA_DK, GLA_DV)
    return (y_prompt, y_sample, new_state)
```

```python
import functools
import math

import numpy as np
import jax
import jax.numpy as jnp
from jax import lax
from jax.experimental import pallas as pl
from jax.experimental.pallas import tpu as pltpu

F32 = jnp.float32
BF16 = jnp.bfloat16

D_MODEL = 1024
D_FF = 2816
N_MOD = 9
EPS = 1e-6
GLA_HEADS = 4
GLA_DK = 64
GLA_DV = 128
QK_W = GLA_HEADS * GLA_DK
GLA_WIDTH = GLA_HEADS * GLA_DV
CONV_WIDTH = 512
GLA_LOWRANK = 16
GLA_TAU = 16.0
GRID_W = 64
SPLITS = [QK_W, QK_W, GLA_WIDTH, GLA_WIDTH, GLA_LOWRANK, GLA_LOWRANK, CONV_WIDTH, CONV_WIDTH, CONV_WIDTH]

LANES = 128
ROW_GROUP = 16
CHUNK = 128
N_LEVELS = 7
N_HALF_LEVELS = 3
TILE_FFN = 512
FF_CHUNKS = ((0, D_FF),)
TILE_MIX = 256
FFN_LEAD = 1
LR_PAD = LANES
QKV_W = 2 * QK_W + GLA_WIDTH
VMEM_LIMIT_FFN = 56 * 1024 * 1024
VMEM_LIMIT_MIX = 48 * 1024 * 1024
LOG2E = math.log2(math.e)

_NT = (((1,), (1,)), ((), ()))


def _dot(a, b):
    return jnp.dot(a, b, preferred_element_type=F32)


def _dot_nt(a, b):
    return lax.dot_general(a, b, _NT, preferred_element_type=F32)


def _silu(x):
    return x / (1.0 + jnp.exp(-x))


def _rms(x):
    return x * lax.rsqrt(jnp.mean(x * x, axis=-1, keepdims=True) + EPS)


def _rms_mod(x, gain, mod_ref, base):
    return _rms(x) * gain * (1.0 + mod_ref[base + 1:base + 2, :]) + mod_ref[base:base + 1, :]


def _const_spec(shape):
    nd = len(shape)
    return pl.BlockSpec(shape, lambda *_: (0,) * nd, pipeline_mode=pl.Buffered(1))


def _ada_kernel(c_ref, w_ref, b_ref, o_ref):
    a = _silu(c_ref[...]).astype(BF16)
    o_ref[...] = _dot(a, w_ref[...].astype(BF16)) + b_ref[...]


def _ada_mod(cvec, w, b):
    rows = cvec.shape[0]
    n = w.shape[1]
    tn = D_MODEL
    return pl.pallas_call(
        _ada_kernel,
        grid=(n // tn,),
        in_specs=[
            pl.BlockSpec((rows, D_MODEL), lambda i: (0, 0)),
            pl.BlockSpec((D_MODEL, tn), lambda i: (0, i)),
            pl.BlockSpec((1, tn), lambda i: (0, i)),
        ],
        out_specs=pl.BlockSpec((rows, tn), lambda i: (0, i)),
        out_shape=jax.ShapeDtypeStruct((rows, n), F32),
        name="ada_mod",
    )(cvec, w, b.reshape(1, n))


def _interleave(generators, lead=0):
    active = list(generators)
    for _ in range(lead):
        next(active[0])
    while active:
        for g in list(active):
            try:
                next(g)
            except StopIteration:
                active.remove(g)


def _ffn_core(x, mod_ref, base, norm_ref, w1_ref, w3_ref, w2_ref, out):
    h = _rms_mod(x, norm_ref[...], mod_ref, base).astype(BF16)
    yield
    acc = None
    for c0, c1 in FF_CHUNKS:
        a = _dot(h, w1_ref[:, c0:c1])
        b = _dot(h, w3_ref[:, c0:c1])
        yield
        part = _dot((_silu(a) * b).astype(BF16), w2_ref[c0:c1, :])
        acc = part if acc is None else acc + part
        yield
    out.append(x + (0.5 * mod_ref[base + 2:base + 3, :]) * acc)


def _row_halves(n):
    half = n // 2
    return [slice(0, half), slice(half, n)]


def _ffn_proj_kernel(x_ref, mod_ref, n1_ref, w1_ref, w3_ref, w2_ref, n2_ref, wqkv_ref, wlr_ref,
                     xo_ref, qkv_ref, lr_ref):
    def half(rows):
        out = []
        yield from _ffn_core(x_ref[rows, :], mod_ref, 0, n1_ref, w1_ref, w3_ref, w2_ref, out)
        y = out[0]
        xo_ref[rows, :] = y
        h2 = _rms_mod(y, n2_ref[...], mod_ref, 3).astype(BF16)
        yield
        qkv_ref[rows, :] = _dot(h2, wqkv_ref[...]).astype(BF16)
        yield
        lr_ref[rows, :] = _dot(h2, wlr_ref[...]).astype(BF16)

    _interleave([half(rows) for rows in _row_halves(x_ref.shape[0])], lead=FFN_LEAD)


def _mod_spec(mod_base, rows_per_mod, tile):
    per = rows_per_mod // tile
    return pl.BlockSpec((None, N_MOD, D_MODEL), lambda i: (mod_base + i // per, 0, 0))


def _ffn_weight_specs():
    return [_const_spec((1, D_MODEL)), _const_spec((D_MODEL, D_FF)), _const_spec((D_MODEL, D_FF)),
            _const_spec((D_FF, D_MODEL))]


def _ffn_proj(x, mods, mod_base, rows_per_mod, lw):
    n = x.shape[0]
    tm = TILE_FFN
    row = lambda i: (i, 0)
    return pl.pallas_call(
        _ffn_proj_kernel,
        grid=(n // tm,),
        in_specs=[pl.BlockSpec((tm, D_MODEL), row), _mod_spec(mod_base, rows_per_mod, tm)]
        + _ffn_weight_specs()
        + [_const_spec((1, D_MODEL)), _const_spec((D_MODEL, QKV_W)), _const_spec((D_MODEL, LR_PAD))],
        out_specs=[
            pl.BlockSpec((tm, D_MODEL), row),
            pl.BlockSpec((tm, QKV_W), row),
            pl.BlockSpec((tm, LR_PAD), row),
        ],
        out_shape=[
            jax.ShapeDtypeStruct((n, D_MODEL), F32),
            jax.ShapeDtypeStruct((n, QKV_W), BF16),
            jax.ShapeDtypeStruct((n, LR_PAD), BF16),
        ],
        compiler_params=pltpu.CompilerParams(
            dimension_semantics=("arbitrary",), vmem_limit_bytes=VMEM_LIMIT_FFN),
        name="ffn1_proj",
    )(x, mods, lw["n1"], lw["w1a"], lw["w3a"], lw["w2a"], lw["n_mix"], lw["w_qkv"], lw["w_lr"])


def _gla_tables():
    c = CHUNK
    t = np.arange(c)[:, None]
    u = np.arange(c)[None, :]
    dsum = np.zeros((2, N_LEVELS + 1, c, c), np.float32)
    mask = np.zeros((2, N_LEVELS + 1, c, c), np.float32)
    drest = np.zeros((2, c, c), np.float32)
    for lvl in range(N_LEVELS):
        m = c >> (lvl + 1)
        mid = (t // (2 * m)) * (2 * m) + m
        mid_u = (u // (2 * m)) * (2 * m) + m
        same = (t // (2 * m)) == (u // (2 * m))
        hi_t, hi_u = t >= mid, u >= mid_u
        dsum[0, lvl] = np.where(hi_t, (u >= mid) & (u <= t), (u > t) & (u < mid))
        mask[0, lvl] = same & hi_t & ~hi_u
        dsum[1, lvl] = np.where(hi_t, (u >= mid) & (u < t), (u >= t) & (u < mid))
        mask[1, lvl] = same & ~hi_t & hi_u
    dsum[0, N_LEVELS] = u <= t
    dsum[1, N_LEVELS] = u >= t
    mask[:, N_LEVELS] = t == u
    drest[0] = u > t
    drest[1] = u < t
    dsum = dsum.reshape(2, (N_LEVELS + 1) * c, c)
    dsum2 = np.concatenate([dsum, dsum], axis=2)
    rest_t = np.transpose(drest, (0, 2, 1))
    rest1 = np.concatenate([rest_t, np.ones_like(rest_t)], axis=2)
    rest2 = np.concatenate([rest1, rest1], axis=1)
    mask2 = np.concatenate([mask, mask], axis=2)
    return (jnp.asarray(dsum2, BF16), jnp.asarray(rest2, BF16), jnp.asarray(mask2, BF16))


def _query_rows(lvl, d):
    if lvl >= N_HALF_LEVELS:
        return [(0, 2 * CHUNK)]
    m = CHUNK >> (lvl + 1)
    assert m % ROW_GROUP == 0
    lo = m if d == 0 else 0
    return [(hh * CHUNK + b0 + lo, hh * CHUNK + b0 + lo + m)
            for hh in range(2) for b0 in range(0, CHUNK, 2 * m)]


def _split_hi_lo(x):
    hi = x.astype(BF16)
    lo = (x - hi.astype(F32)).astype(BF16)
    return hi, lo


def _log2_decay(x):
    log_sigmoid = jnp.minimum(x, 0.0) - jnp.log(1.0 + jnp.exp(-jnp.abs(x)))
    return log_sigmoid * (LOG2E / GLA_TAU)


def _rows(x, ranges):
    parts = [x[r0:r1] for r0, r1 in ranges]
    return parts[0] if len(parts) == 1 else jnp.concatenate(parts, axis=0)


def _gla_chunk(d, q, k, v, kt, la_hl, lat_hl, dsum_ref, rest_ref, mask_ref, state, outs):
    e_all = jnp.exp2(_dot(dsum_ref[...], la_hl)).astype(BF16)
    yield
    r = _dot(lat_hl, rest_ref[...])
    k_rest_t = kt * jnp.exp2(r[:, :CHUNK]).astype(BF16)
    gamma = jnp.exp2(r[:, CHUNK:])
    lane = lax.broadcasted_iota(jnp.int32, (CHUNK, LANES), 1)
    zero = jnp.zeros((), BF16)
    upd = []
    for pr in range(GLA_HEADS // 2):
        ls = slice(LANES * pr, LANES * (pr + 1))
        qp, kp = q[:, ls], k[:, ls]
        qm = jnp.concatenate([jnp.where(lane < GLA_DK, qp, zero),
                              jnp.where(lane >= GLA_DK, qp, zero)], axis=0)
        diag = _dot_nt(qm, kp).astype(BF16) * mask_ref[N_LEVELS]
        groups = [diag[g0:g0 + ROW_GROUP] for g0 in range(0, 2 * CHUNK, ROW_GROUP)]
        yield
        for lvl in range(N_LEVELS):
            e = e_all[CHUNK * lvl:CHUNK * (lvl + 1), ls]
            ranges = _query_rows(lvl, d)
            lhs = _rows(qm, ranges) * _rows(jnp.concatenate([e, e], axis=0), ranges)
            part = _dot_nt(lhs, kp * e).astype(BF16)
            off = 0
            for r0, r1 in ranges:
                for g0 in range(r0, r1, ROW_GROUP):
                    groups[g0 // ROW_GROUP] += part[off:off + ROW_GROUP] * mask_ref[lvl, g0:g0 + ROW_GROUP, :]
                    off += ROW_GROUP
            yield
        scores = jnp.concatenate(groups, axis=0)
        e = e_all[CHUNK * N_LEVELS:, ls]
        q_in = qm * jnp.concatenate([e, e], axis=0)
        o_inter = _dot(q_in, state[0][LANES * pr:LANES * (pr + 1), :].astype(BF16))
        for hh in range(2):
            h = 2 * pr + hh
            vh = v[:, GLA_DV * h:GLA_DV * (h + 1)]
            rows = slice(CHUNK * hh, CHUNK * (hh + 1))
            outs.append(o_inter[rows] + _dot(scores[rows], vh))
            upd.append(_dot(k_rest_t[GLA_DK * h:GLA_DK * (h + 1), :], vh))
        yield
    state[0] = gamma * state[0] + jnp.concatenate(upd, axis=0)


def _gla_kernel(*refs, nt, has_s0):
    ins, refs = refs[:6], refs[6:]
    if has_s0:
        s0_ref, refs = refs[0], refs[1:]
    (wd_ref, bd_ref, dsum_ref, rest_ref, mask_ref,
     of_ref, ob_ref, st_ref, s_ref) = refs
    n_chunks = TILE_MIX // CHUNK

    @pl.when(pl.program_id(1) == 0)
    def _():
        s_ref[...] = s0_ref[...] if has_s0 else jnp.zeros(s_ref.shape, F32)

    def scan(d):
        qk_ref, v_ref, lr_ref = ins[3 * d:3 * d + 3]
        o_ref = (of_ref, ob_ref)[d]
        lr = lr_ref[...]
        la = _log2_decay(_dot(lr, wd_ref[d]) + bd_ref[d])
        la_hi, la_lo = _split_hi_lo(la)
        lat_hi, lat_lo = _split_hi_lo(la.T)
        q_all = qk_ref[:, :QK_W] * jnp.asarray(GLA_DK ** -0.5, BF16)
        k_all = qk_ref[:, QK_W:]
        kt_all = k_all.T
        state = [s_ref[d]]
        yield
        for c in (range(n_chunks) if d == 0 else reversed(range(n_chunks))):
            rows = slice(CHUNK * c, CHUNK * (c + 1))
            la_hl = jnp.concatenate([la_hi[rows], la_lo[rows]], axis=0)
            lat_hl = jnp.concatenate([lat_hi[:, rows], lat_lo[:, rows]], axis=1)
            outs = []
            yield from _gla_chunk(d, q_all[rows], k_all[rows], v_ref[rows, :], kt_all[:, rows], la_hl, lat_hl,
                                  dsum_ref.at[d], rest_ref.at[d], mask_ref.at[d], state, outs)
            for h in range(GLA_HEADS):
                o_ref[rows, GLA_DV * h:GLA_DV * (h + 1)] = outs[h]
        s_ref[d] = state[0]
        st_ref[d] = state[0]

    _interleave([scan(0), scan(1)])


def _gla(qkv, lr, s0, seq_len, lw):
    n = qkv.shape[0]
    batch = n // seq_len
    t = TILE_MIX
    nt = seq_len // t
    tile = (lambda b, j: b * nt + j, lambda b, j: b * nt + nt - 1 - j)
    in_specs, args = [], []
    for d in range(2):
        in_specs += [
            pl.BlockSpec((t, 2 * QK_W), lambda b, j, d=d: (tile[d](b, j), 0)),
            pl.BlockSpec((t, GLA_WIDTH), lambda b, j, d=d: (tile[d](b, j), 1)),
            pl.BlockSpec((t, LR_PAD), lambda b, j, d=d: (tile[d](b, j), 0)),
        ]
        args += [qkv, qkv, lr]
    state_spec = pl.BlockSpec((None, 2, QK_W, GLA_DV), lambda b, j: (b, 0, 0, 0))
    if s0 is not None:
        in_specs.append(state_spec)
        args.append(s0)
    tables = [lw["wd"], lw["bd"]] + list(lw["tables"])
    in_specs += [_const_spec(a.shape) for a in tables]
    args += tables
    return pl.pallas_call(
        functools.partial(_gla_kernel, nt=nt, has_s0=s0 is not None),
        grid=(batch, nt),
        in_specs=in_specs,
        out_specs=[
            pl.BlockSpec((t, GLA_WIDTH), lambda b, j: (tile[0](b, j), 0)),
            pl.BlockSpec((t, GLA_WIDTH), lambda b, j: (tile[1](b, j), 0)),
            state_spec,
        ],
        out_shape=[
            jax.ShapeDtypeStruct((n, GLA_WIDTH), F32),
            jax.ShapeDtypeStruct((n, GLA_WIDTH), F32),
            jax.ShapeDtypeStruct((batch, 2, QK_W, GLA_DV), F32),
        ],
        scratch_shapes=[pltpu.VMEM((2, QK_W, GLA_DV), F32)],
        compiler_params=pltpu.CompilerParams(
            dimension_semantics=("arbitrary", "arbitrary"), vmem_limit_bytes=VMEM_LIMIT_MIX),
        name="gla",
    )(*args)


def _mix_ffn_kernel(x_ref, of_ref, ob_ref, mod_ref, nm_ref, wg_ref, wc_ref, gn_ref, cw_ref, wout_ref,
                    n2_ref, w1_ref, w3_ref, w2_ref, fn_ref, yo_ref, *, period):
    def half(rows):
        x = x_ref[rows, :]
        h2 = _rms_mod(x, nm_ref[...], mod_ref, 3).astype(BF16)
        o = of_ref[rows, :] + ob_ref[rows, :]
        yield
        g = _dot(h2, wg_ref[...])
        parts = []
        for h in range(GLA_HEADS):
            cols = slice(GLA_DV * h, GLA_DV * (h + 1))
            parts.append(_rms(o[:, cols]) * gn_ref[:, cols] * _silu(g[:, cols]))
        yield
        conv = _dot(h2, wc_ref[...])
        cb = conv[:, :CONV_WIDTH]
        u = conv[:, CONV_WIDTH:2 * CONV_WIDTH] * conv[:, 2 * CONV_WIDTH:]
        pos = lax.broadcasted_iota(jnp.int32, u.shape, 0) & (period - 1)
        u_prev = jnp.where(pos == 0, 0.0, pltpu.roll(u, 1, 0))
        u_next = jnp.where(pos == period - 1, 0.0, pltpu.roll(u, u.shape[0] - 1, 0))
        cu = cw_ref[0:1, :] * u_prev + cw_ref[1:2, :] * u + cw_ref[2:3, :] * u_next
        parts.append(cb * cu)
        mixed = jnp.concatenate(parts, axis=-1).astype(BF16)
        yield
        x2 = x + mod_ref[5:6, :] * _dot(mixed, wout_ref[...])
        out = []
        yield from _ffn_core(x2, mod_ref, 6, n2_ref, w1_ref, w3_ref, w2_ref, out)
        yo_ref[rows, :] = _rms(out[0]) * fn_ref[...]

    _interleave([half(rows) for rows in _row_halves(x_ref.shape[0])], lead=FFN_LEAD)


def _mix_ffn(x, o_fwd, o_bwd, mods, mod_base, rows_per_mod, period, lw, fn):
    n = x.shape[0]
    tm = TILE_FFN
    assert period & (period - 1) == 0 and (tm // 2) % period == 0
    row = lambda i: (i, 0)
    return pl.pallas_call(
        functools.partial(_mix_ffn_kernel, period=period),
        grid=(n // tm,),
        in_specs=[pl.BlockSpec((tm, D_MODEL), row), pl.BlockSpec((tm, GLA_WIDTH), row),
                  pl.BlockSpec((tm, GLA_WIDTH), row), _mod_spec(mod_base, rows_per_mod, tm),
                  _const_spec((1, D_MODEL)), _const_spec((D_MODEL, GLA_WIDTH)),
                  _const_spec((D_MODEL, 3 * CONV_WIDTH)), _const_spec((1, GLA_WIDTH)),
                  _const_spec((3, CONV_WIDTH)), _const_spec((D_MODEL, D_MODEL))]
        + _ffn_weight_specs() + [_const_spec((1, D_MODEL))],
        out_specs=pl.BlockSpec((tm, D_MODEL), row),
        out_shape=jax.ShapeDtypeStruct((n, D_MODEL), F32),
        compiler_params=pltpu.CompilerParams(
            dimension_semantics=("arbitrary",), vmem_limit_bytes=VMEM_LIMIT_FFN),
        name="mix_ffn2",
    )(x, o_fwd, o_bwd, mods, lw["n_mix"], lw["w_g"], lw["w_conv"], lw["gla_norm"], lw["conv_w"], lw["w_out"],
      lw["n2"], lw["w1b"], lw["w3b"], lw["w2b"], fn)


def _split_w_in_kernel(wt_ref, qkv_ref, g_ref, conv_ref, lr_ref):
    edges = [0] + list(np.cumsum(SPLITS))
    piece = lambda a, b: wt_ref[edges[a]:edges[b], :]
    qkv_ref[...] = piece(0, 3).T.astype(BF16)
    g_ref[...] = piece(3, 4).T.astype(BF16)
    conv_ref[...] = piece(6, 9).T.astype(BF16)
    lr_t = jnp.concatenate([piece(4, 6), jnp.zeros((LR_PAD - 2 * GLA_LOWRANK, wt_ref.shape[1]), F32)], axis=0)
    lr_ref[...] = lr_t.T.astype(BF16)


def _split_w_in(w_in, l):
    rows = 256
    n_cols = w_in.shape[2]
    row = lambda i: (i, 0)
    return pl.pallas_call(
        _split_w_in_kernel,
        grid=(D_MODEL // rows,),
        in_specs=[pl.BlockSpec((n_cols, rows), lambda i: (0, i))],
        out_specs=[pl.BlockSpec((rows, QKV_W), row), pl.BlockSpec((rows, GLA_WIDTH), row),
                   pl.BlockSpec((rows, 3 * CONV_WIDTH), row), pl.BlockSpec((rows, LR_PAD), row)],
        out_shape=[jax.ShapeDtypeStruct((D_MODEL, QKV_W), BF16), jax.ShapeDtypeStruct((D_MODEL, GLA_WIDTH), BF16),
                   jax.ShapeDtypeStruct((D_MODEL, 3 * CONV_WIDTH), BF16),
                   jax.ShapeDtypeStruct((D_MODEL, LR_PAD), BF16)],
        name="split_w_in",
    )(jnp.transpose(w_in[l]))


def _prep_layer(l, norm_ffn1, w1_ffn1, w3_ffn1, w2_ffn1, norm_mix, w_in, w_decay, b_decay, gla_norm,
                conv_w, w_out, norm_ffn2, w1_ffn2, w3_ffn2, w2_ffn2, tables):
    w_qkv, w_g, w_conv, w_lr = _split_w_in(w_in, l)
    wd = jnp.zeros((2, LR_PAD, QK_W), F32)
    wd = wd.at[0, :GLA_LOWRANK].set(w_decay[l, 0]).at[1, GLA_LOWRANK:2 * GLA_LOWRANK].set(w_decay[l, 1])
    bd = b_decay[l].reshape(2, 1, QK_W)
    return dict(
        n1=norm_ffn1[l].reshape(1, -1), w1a=w1_ffn1[l].astype(BF16), w3a=w3_ffn1[l].astype(BF16),
        w2a=w2_ffn1[l].astype(BF16),
        n_mix=norm_mix[l].reshape(1, -1), w_qkv=w_qkv, w_lr=w_lr, w_g=w_g, w_conv=w_conv,
        wd=wd.astype(BF16), bd=bd,
        gla_norm=gla_norm[l].reshape(1, -1), conv_w=conv_w[l], w_out=w_out[l].astype(BF16),
        n2=norm_ffn2[l].reshape(1, -1), w1b=w1_ffn2[l].astype(BF16), w3b=w3_ffn2[l].astype(BF16),
        w2b=w2_ffn2[l].astype(BF16), tables=tables)


def kernel(x_prompt, x_sample, state_gla, c, c_ctx, w_ada, b_ada, norm_ffn1, w1_ffn1, w3_ffn1, w2_ffn1,
           norm_mix, w_in, w_decay, b_decay, gla_norm, conv_w, w_out, norm_ffn2, w1_ffn2, w3_ffn2,
           w2_ffn2, final_norm):
    batch, seq, _ = x_prompt.shape
    dec_batch, dec_seq, _ = x_sample.shape
    depth = w_ada.shape[0]
    assert depth == 1, "the fused final norm assumes a single layer"
    l = 0
    xp = x_prompt.reshape(batch * seq, D_MODEL)
    xs = x_sample.reshape(dec_batch * dec_seq, D_MODEL)
    mod_rows = 8
    assert 1 + dec_batch <= mod_rows
    cvec = jnp.concatenate([c_ctx[None, :], c, jnp.zeros((mod_rows - 1 - dec_batch, D_MODEL), F32)], axis=0)
    fn = final_norm.reshape(1, -1)
    lw = _prep_layer(l, norm_ffn1, w1_ffn1, w3_ffn1, w2_ffn1, norm_mix, w_in, w_decay, b_decay,
                     gla_norm, conv_w, w_out, norm_ffn2, w1_ffn2, w3_ffn2, w2_ffn2, _gla_tables())
    mods = _ada_mod(cvec, w_ada[l], b_ada[l]).reshape(mod_rows, N_MOD, D_MODEL)
    s0 = state_gla[:, l].reshape(dec_batch, 2, QK_W, GLA_DV)
    results = []
    for (x, mod_base, rows_per_mod, init, slen, period) in (
            (xp, 0, batch * seq, None, seq, seq),
            (xs, 1, dec_seq, s0, dec_seq, GRID_W)):
        x1, qkv, lr = _ffn_proj(x, mods, mod_base, rows_per_mod, lw)
        o_fwd, o_bwd, st = _gla(qkv, lr, init, slen, lw)
        results.append((_mix_ffn(x1, o_fwd, o_bwd, mods, mod_base, rows_per_mod, period, lw, fn), st))
    (yp, st_p), (ys, _) = results
    y_prompt = yp.reshape(batch, seq, D_MODEL)
    y_sample = ys.reshape(dec_batch, dec_seq, D_MODEL)
    new_state = st_p.reshape(batch, 1, 2, GLA_HEADS, GLA_DK, GLA_DV)
    return (y_prompt, y_sample, new_state)
```

```python
import functools
import math

import numpy as np
import jax
import jax.numpy as jnp
from jax import lax
from jax.experimental import pallas as pl
from jax.experimental.pallas import tpu as pltpu

F32 = jnp.float32
BF16 = jnp.bfloat16

D_MODEL = 1024
D_FF = 2816
N_MOD = 9
EPS = 1e-6
GLA_HEADS = 4
GLA_DK = 64
GLA_DV = 128
QK_W = GLA_HEADS * GLA_DK
GLA_WIDTH = GLA_HEADS * GLA_DV
CONV_WIDTH = 512
GLA_LOWRANK = 16
GLA_TAU = 16.0
GRID_W = 64
SPLITS = [QK_W, QK_W, GLA_WIDTH, GLA_WIDTH, GLA_LOWRANK, GLA_LOWRANK, CONV_WIDTH, CONV_WIDTH, CONV_WIDTH]

LANES = 128
ROW_GROUP = 16
CHUNK = 128
N_LEVELS = 7
N_HALF_LEVELS = 3
TILE_FFN = 512
FF_CHUNKS = ((0, D_FF),)
TILE_MIX = 256
FFN_LEAD = 1
LR_PAD = LANES
QKV_W = 2 * QK_W + GLA_WIDTH
VMEM_LIMIT_FFN = 56 * 1024 * 1024
VMEM_LIMIT_MIX = 48 * 1024 * 1024
LOG2E = math.log2(math.e)

_NT = (((1,), (1,)), ((), ()))


def _dot(a, b):
    return jnp.dot(a, b, preferred_element_type=F32)


def _dot_nt(a, b):
    return lax.dot_general(a, b, _NT, preferred_element_type=F32)


def _silu(x):
    return x / (1.0 + jnp.exp(-x))


def _rms(x):
    return x * lax.rsqrt(jnp.mean(x * x, axis=-1, keepdims=True) + EPS)


def _rms_mod(x, gain, mod_ref, base):
    return _rms(x) * gain * (1.0 + mod_ref[base + 1:base + 2, :]) + mod_ref[base:base + 1, :]


def _const_spec(shape):
    nd = len(shape)
    return pl.BlockSpec(shape, lambda *_: (0,) * nd, pipeline_mode=pl.Buffered(1))


def _ada_kernel(c_ref, w_ref, b_ref, o_ref):
    a = _silu(c_ref[...]).astype(BF16)
    o_ref[...] = _dot(a, w_ref[...].astype(BF16)) + b_ref[...]


def _ada_mod(cvec, w, b):
    rows = cvec.shape[0]
    n = w.shape[1]
    tn = D_MODEL
    return pl.pallas_call(
        _ada_kernel,
        grid=(n // tn,),
        in_specs=[
            pl.BlockSpec((rows, D_MODEL), lambda i: (0, 0)),
            pl.BlockSpec((D_MODEL, tn), lambda i: (0, i)),
            pl.BlockSpec((1, tn), lambda i: (0, i)),
        ],
        out_specs=pl.BlockSpec((rows, tn), lambda i: (0, i)),
        out_shape=jax.ShapeDtypeStruct((rows, n), F32),
        name="ada_mod",
    )(cvec, w, b.reshape(1, n))


def _interleave(generators, lead=0):
    active = list(generators)
    for _ in range(lead):
        next(active[0])
    while active:
        for g in list(active):
            try:
                next(g)
            except StopIteration:
                active.remove(g)


def _ffn_core(x, mod_ref, base, norm_ref, w1_ref, w3_ref, w2_ref, out):
    h = _rms_mod(x, norm_ref[...], mod_ref, base).astype(BF16)
    yield
    acc = None
    for c0, c1 in FF_CHUNKS:
        a = _dot(h, w1_ref[:, c0:c1])
        b = _dot(h, w3_ref[:, c0:c1])
        yield
        part = _dot((_silu(a) * b).astype(BF16), w2_ref[c0:c1, :])
        acc = part if acc is None else acc + part
        yield
    out.append(x + (0.5 * mod_ref[base + 2:base + 3, :]) * acc)


def _row_halves(n):
    half = n // 2
    return [slice(0, half), slice(half, n)]


def _ffn_proj_kernel(x_ref, mod_ref, n1_ref, w1_ref, w3_ref, w2_ref, n2_ref, wqkv_ref, wlr_ref, *rest):
    n_cast = (len(rest) - 3) // 2
    cast_in, (xo_ref, qkv_ref, lr_ref), cast_out = rest[:n_cast], rest[n_cast:n_cast + 3], rest[n_cast + 3:]
    for src, dst in zip(cast_in, cast_out):
        dst[...] = src[...].astype(BF16)

    def half(rows):
        out = []
        yield from _ffn_core(x_ref[rows, :], mod_ref, 0, n1_ref, w1_ref, w3_ref, w2_ref, out)
        y = out[0]
        xo_ref[rows, :] = y
        h2 = _rms_mod(y, n2_ref[...], mod_ref, 3).astype(BF16)
        yield
        qkv_ref[rows, :] = _dot(h2, wqkv_ref[...]).astype(BF16)
        yield
        lr_ref[rows, :] = _dot(h2, wlr_ref[...]).astype(BF16)

    _interleave([half(rows) for rows in _row_halves(x_ref.shape[0])], lead=FFN_LEAD)


def _mod_spec(mod_base, rows_per_mod, tile):
    per = rows_per_mod // tile
    return pl.BlockSpec((None, N_MOD, D_MODEL), lambda i: (mod_base + i // per, 0, 0))


def _ffn_weight_specs():
    return [_const_spec((1, D_MODEL)), _const_spec((D_MODEL, D_FF)), _const_spec((D_MODEL, D_FF)),
            _const_spec((D_FF, D_MODEL))]


def _ffn_proj(x, mods, mod_base, rows_per_mod, lw, to_cast=()):
    n = x.shape[0]
    tm = TILE_FFN
    steps = n // tm
    row = lambda i: (i, 0)
    cast_specs = []
    for w in to_cast:
        assert w.shape[0] % (steps * ROW_GROUP) == 0
        cast_specs.append(pl.BlockSpec((w.shape[0] // steps, w.shape[1]), row))
    return pl.pallas_call(
        _ffn_proj_kernel,
        grid=(steps,),
        in_specs=[pl.BlockSpec((tm, D_MODEL), row), _mod_spec(mod_base, rows_per_mod, tm)]
        + _ffn_weight_specs()
        + [_const_spec((1, D_MODEL)), _const_spec((D_MODEL, QKV_W)), _const_spec((D_MODEL, LR_PAD))]
        + cast_specs,
        out_specs=[
            pl.BlockSpec((tm, D_MODEL), row),
            pl.BlockSpec((tm, QKV_W), row),
            pl.BlockSpec((tm, LR_PAD), row),
        ] + cast_specs,
        out_shape=[
            jax.ShapeDtypeStruct((n, D_MODEL), F32),
            jax.ShapeDtypeStruct((n, QKV_W), BF16),
            jax.ShapeDtypeStruct((n, LR_PAD), BF16),
        ] + [jax.ShapeDtypeStruct(w.shape, BF16) for w in to_cast],
        compiler_params=pltpu.CompilerParams(
            dimension_semantics=("arbitrary",), vmem_limit_bytes=VMEM_LIMIT_FFN),
        name="ffn1_proj",
    )(x, mods, lw["n1"], lw["w1a"], lw["w3a"], lw["w2a"], lw["n_mix"], lw["w_qkv"], lw["w_lr"], *to_cast)


def _gla_tables():
    c = CHUNK
    t = np.arange(c)[:, None]
    u = np.arange(c)[None, :]
    dsum = np.zeros((2, N_LEVELS + 1, c, c), np.float32)
    mask = np.zeros((2, N_LEVELS + 1, c, c), np.float32)
    drest = np.zeros((2, c, c), np.float32)
    for lvl in range(N_LEVELS):
        m = c >> (lvl + 1)
        mid = (t // (2 * m)) * (2 * m) + m
        mid_u = (u // (2 * m)) * (2 * m) + m
        same = (t // (2 * m)) == (u // (2 * m))
        hi_t, hi_u = t >= mid, u >= mid_u
        dsum[0, lvl] = np.where(hi_t, (u >= mid) & (u <= t), (u > t) & (u < mid))
        mask[0, lvl] = same & hi_t & ~hi_u
        dsum[1, lvl] = np.where(hi_t, (u >= mid) & (u < t), (u >= t) & (u < mid))
        mask[1, lvl] = same & ~hi_t & hi_u
    dsum[0, N_LEVELS] = u <= t
    dsum[1, N_LEVELS] = u >= t
    mask[:, N_LEVELS] = t == u
    drest[0] = u > t
    drest[1] = u < t
    dsum = dsum.reshape(2, (N_LEVELS + 1) * c, c)
    dsum2 = np.concatenate([dsum, dsum], axis=2)
    rest_t = np.transpose(drest, (0, 2, 1))
    rest1 = np.concatenate([rest_t, np.ones_like(rest_t)], axis=2)
    rest2 = np.concatenate([rest1, rest1], axis=1)
    mask2 = np.concatenate([mask, mask], axis=2)
    return (jnp.asarray(dsum2, BF16), jnp.asarray(rest2, BF16), jnp.asarray(mask2, BF16))


def _query_rows(lvl, d):
    if lvl >= N_HALF_LEVELS:
        return [(0, 2 * CHUNK)]
    m = CHUNK >> (lvl + 1)
    assert m % ROW_GROUP == 0
    lo = m if d == 0 else 0
    return [(hh * CHUNK + b0 + lo, hh * CHUNK + b0 + lo + m)
            for hh in range(2) for b0 in range(0, CHUNK, 2 * m)]


def _split_hi_lo(x):
    hi = x.astype(BF16)
    lo = (x - hi.astype(F32)).astype(BF16)
    return hi, lo


def _log2_decay(x):
    log_sigmoid = jnp.minimum(x, 0.0) - jnp.log(1.0 + jnp.exp(-jnp.abs(x)))
    return log_sigmoid * (LOG2E / GLA_TAU)


def _rows(x, ranges):
    parts = [x[r0:r1] for r0, r1 in ranges]
    return parts[0] if len(parts) == 1 else jnp.concatenate(parts, axis=0)


def _gla_chunk(d, q, k, v, kt, la_hl, lat_hl, dsum_ref, rest_ref, mask_ref, state, outs):
    e_all = jnp.exp2(_dot(dsum_ref[...], la_hl)).astype(BF16)
    yield
    r = _dot(lat_hl, rest_ref[...])
    k_rest_t = kt * jnp.exp2(r[:, :CHUNK]).astype(BF16)
    gamma = jnp.exp2(r[:, CHUNK:])
    lane = lax.broadcasted_iota(jnp.int32, (CHUNK, LANES), 1)
    zero = jnp.zeros((), BF16)
    upd = []
    for pr in range(GLA_HEADS // 2):
        ls = slice(LANES * pr, LANES * (pr + 1))
        qp, kp = q[:, ls], k[:, ls]
        qm = jnp.concatenate([jnp.where(lane < GLA_DK, qp, zero),
                              jnp.where(lane >= GLA_DK, qp, zero)], axis=0)
        diag = _dot_nt(qm, kp).astype(BF16) * mask_ref[N_LEVELS]
        groups = [diag[g0:g0 + ROW_GROUP] for g0 in range(0, 2 * CHUNK, ROW_GROUP)]
        yield
        for lvl in range(N_LEVELS):
            e = e_all[CHUNK * lvl:CHUNK * (lvl + 1), ls]
            ranges = _query_rows(lvl, d)
            lhs = _rows(qm, ranges) * _rows(jnp.concatenate([e, e], axis=0), ranges)
            part = _dot_nt(lhs, kp * e).astype(BF16)
            off = 0
            for r0, r1 in ranges:
                for g0 in range(r0, r1, ROW_GROUP):
                    groups[g0 // ROW_GROUP] += part[off:off + ROW_GROUP] * mask_ref[lvl, g0:g0 + ROW_GROUP, :]
                    off += ROW_GROUP
            yield
        scores = jnp.concatenate(groups, axis=0)
        e = e_all[CHUNK * N_LEVELS:, ls]
        q_in = qm * jnp.concatenate([e, e], axis=0)
        o_inter = _dot(q_in, state[0][LANES * pr:LANES * (pr + 1), :].astype(BF16))
        for hh in range(2):
            h = 2 * pr + hh
            vh = v[:, GLA_DV * h:GLA_DV * (h + 1)]
            rows = slice(CHUNK * hh, CHUNK * (hh + 1))
            outs.append(o_inter[rows] + _dot(scores[rows], vh))
            upd.append(_dot(k_rest_t[GLA_DK * h:GLA_DK * (h + 1), :], vh))
        yield
    state[0] = gamma * state[0] + jnp.concatenate(upd, axis=0)


def _gla_kernel(*refs, nt, has_s0):
    ins, refs = refs[:6], refs[6:]
    if has_s0:
        s0_ref, refs = refs[0], refs[1:]
    (wd_ref, bd_ref, dsum_ref, rest_ref, mask_ref,
     of_ref, ob_ref, st_ref, s_ref) = refs
    n_chunks = TILE_MIX // CHUNK

    @pl.when(pl.program_id(1) == 0)
    def _():
        s_ref[...] = s0_ref[...] if has_s0 else jnp.zeros(s_ref.shape, F32)

    def scan(d):
        qk_ref, v_ref, lr_ref = ins[3 * d:3 * d + 3]
        o_ref = (of_ref, ob_ref)[d]
        lr = lr_ref[...]
        la = _log2_decay(_dot(lr, wd_ref[d]) + bd_ref[d])
        la_hi, la_lo = _split_hi_lo(la)
        lat_hi, lat_lo = _split_hi_lo(la.T)
        q_all = qk_ref[:, :QK_W] * jnp.asarray(GLA_DK ** -0.5, BF16)
        k_all = qk_ref[:, QK_W:]
        kt_all = k_all.T
        state = [s_ref[d]]
        yield
        for c in (range(n_chunks) if d == 0 else reversed(range(n_chunks))):
            rows = slice(CHUNK * c, CHUNK * (c + 1))
            la_hl = jnp.concatenate([la_hi[rows], la_lo[rows]], axis=0)
            lat_hl = jnp.concatenate([lat_hi[:, rows], lat_lo[:, rows]], axis=1)
            outs = []
            yield from _gla_chunk(d, q_all[rows], k_all[rows], v_ref[rows, :], kt_all[:, rows], la_hl, lat_hl,
                                  dsum_ref.at[d], rest_ref.at[d], mask_ref.at[d], state, outs)
            for h in range(GLA_HEADS):
                o_ref[rows, GLA_DV * h:GLA_DV * (h + 1)] = outs[h]
        s_ref[d] = state[0]
        st_ref[d] = state[0]

    _interleave([scan(0), scan(1)])


def _gla(qkv, lr, s0, seq_len, lw):
    n = qkv.shape[0]
    batch = n // seq_len
    t = TILE_MIX
    nt = seq_len // t
    tile = (lambda b, j: b * nt + j, lambda b, j: b * nt + nt - 1 - j)
    in_specs, args = [], []
    for d in range(2):
        in_specs += [
            pl.BlockSpec((t, 2 * QK_W), lambda b, j, d=d: (tile[d](b, j), 0)),
            pl.BlockSpec((t, GLA_WIDTH), lambda b, j, d=d: (tile[d](b, j), 1)),
            pl.BlockSpec((t, LR_PAD), lambda b, j, d=d: (tile[d](b, j), 0)),
        ]
        args += [qkv, qkv, lr]
    state_spec = pl.BlockSpec((None, 2, QK_W, GLA_DV), lambda b, j: (b, 0, 0, 0))
    if s0 is not None:
        in_specs.append(state_spec)
        args.append(s0)
    tables = [lw["wd"], lw["bd"]] + list(lw["tables"])
    in_specs += [_const_spec(a.shape) for a in tables]
    args += tables
    return pl.pallas_call(
        functools.partial(_gla_kernel, nt=nt, has_s0=s0 is not None),
        grid=(batch, nt),
        in_specs=in_specs,
        out_specs=[
            pl.BlockSpec((t, GLA_WIDTH), lambda b, j: (tile[0](b, j), 0)),
            pl.BlockSpec((t, GLA_WIDTH), lambda b, j: (tile[1](b, j), 0)),
            state_spec,
        ],
        out_shape=[
            jax.ShapeDtypeStruct((n, GLA_WIDTH), F32),
            jax.ShapeDtypeStruct((n, GLA_WIDTH), F32),
            jax.ShapeDtypeStruct((batch, 2, QK_W, GLA_DV), F32),
        ],
        scratch_shapes=[pltpu.VMEM((2, QK_W, GLA_DV), F32)],
        compiler_params=pltpu.CompilerParams(
            dimension_semantics=("arbitrary", "arbitrary"), vmem_limit_bytes=VMEM_LIMIT_MIX),
        name="gla",
    )(*args)


def _mix_ffn_kernel(x_ref, of_ref, ob_ref, mod_ref, nm_ref, wg_ref, wc_ref, gn_ref, cw_ref, wout_ref,
                    n2_ref, w1_ref, w3_ref, w2_ref, fn_ref, yo_ref, *, period):
    def half(rows):
        x = x_ref[rows, :]
        h2 = _rms_mod(x, nm_ref[...], mod_ref, 3).astype(BF16)
        o = of_ref[rows, :] + ob_ref[rows, :]
        yield
        g = _dot(h2, wg_ref[...])
        parts = []
        for h in range(GLA_HEADS):
            cols = slice(GLA_DV * h, GLA_DV * (h + 1))
            parts.append(_rms(o[:, cols]) * gn_ref[:, cols] * _silu(g[:, cols]))
        yield
        conv = _dot(h2, wc_ref[...])
        cb = conv[:, :CONV_WIDTH]
        u = conv[:, CONV_WIDTH:2 * CONV_WIDTH] * conv[:, 2 * CONV_WIDTH:]
        pos = lax.broadcasted_iota(jnp.int32, u.shape, 0) & (period - 1)
        u_prev = jnp.where(pos == 0, 0.0, pltpu.roll(u, 1, 0))
        u_next = jnp.where(pos == period - 1, 0.0, pltpu.roll(u, u.shape[0] - 1, 0))
        cu = cw_ref[0:1, :] * u_prev + cw_ref[1:2, :] * u + cw_ref[2:3, :] * u_next
        parts.append(cb * cu)
        mixed = jnp.concatenate(parts, axis=-1).astype(BF16)
        yield
        x2 = x + mod_ref[5:6, :] * _dot(mixed, wout_ref[...])
        out = []
        yield from _ffn_core(x2, mod_ref, 6, n2_ref, w1_ref, w3_ref, w2_ref, out)
        yo_ref[rows, :] = _rms(out[0]) * fn_ref[...]

    _interleave([half(rows) for rows in _row_halves(x_ref.shape[0])], lead=FFN_LEAD)


def _mix_ffn(x, o_fwd, o_bwd, mods, mod_base, rows_per_mod, period, lw, fn):
    n = x.shape[0]
    tm = TILE_FFN
    assert period & (period - 1) == 0 and (tm // 2) % period == 0
    row = lambda i: (i, 0)
    return pl.pallas_call(
        functools.partial(_mix_ffn_kernel, period=period),
        grid=(n // tm,),
        in_specs=[pl.BlockSpec((tm, D_MODEL), row), pl.BlockSpec((tm, GLA_WIDTH), row),
                  pl.BlockSpec((tm, GLA_WIDTH), row), _mod_spec(mod_base, rows_per_mod, tm),
                  _const_spec((1, D_MODEL)), _const_spec((D_MODEL, GLA_WIDTH)),
                  _const_spec((D_MODEL, 3 * CONV_WIDTH)), _const_spec((1, GLA_WIDTH)),
                  _const_spec((3, CONV_WIDTH)), _const_spec((D_MODEL, D_MODEL))]
        + _ffn_weight_specs() + [_const_spec((1, D_MODEL))],
        out_specs=pl.BlockSpec((tm, D_MODEL), row),
        out_shape=jax.ShapeDtypeStruct((n, D_MODEL), F32),
        compiler_params=pltpu.CompilerParams(
            dimension_semantics=("arbitrary",), vmem_limit_bytes=VMEM_LIMIT_FFN),
        name="mix_ffn2",
    )(x, o_fwd, o_bwd, mods, lw["n_mix"], lw["w_g"], lw["w_conv"], lw["gla_norm"], lw["conv_w"], lw["w_out"],
      lw["n2"], lw["w1b"], lw["w3b"], lw["w2b"], fn)


def _split_w_in_kernel(wt_ref, qkv_ref, g_ref, conv_ref, lr_ref):
    edges = [0] + list(np.cumsum(SPLITS))
    piece = lambda a, b: wt_ref[edges[a]:edges[b], :]
    qkv_ref[...] = piece(0, 3).T.astype(BF16)
    g_ref[...] = piece(3, 4).T.astype(BF16)
    conv_ref[...] = piece(6, 9).T.astype(BF16)
    lr_t = jnp.concatenate([piece(4, 6), jnp.zeros((LR_PAD - 2 * GLA_LOWRANK, wt_ref.shape[1]), F32)], axis=0)
    lr_ref[...] = lr_t.T.astype(BF16)


def _split_w_in(w_in, l):
    rows = 256
    n_cols = w_in.shape[2]
    row = lambda i: (i, 0)
    return pl.pallas_call(
        _split_w_in_kernel,
        grid=(D_MODEL // rows,),
        in_specs=[pl.BlockSpec((n_cols, rows), lambda i: (0, i))],
        out_specs=[pl.BlockSpec((rows, QKV_W), row), pl.BlockSpec((rows, GLA_WIDTH), row),
                   pl.BlockSpec((rows, 3 * CONV_WIDTH), row), pl.BlockSpec((rows, LR_PAD), row)],
        out_shape=[jax.ShapeDtypeStruct((D_MODEL, QKV_W), BF16), jax.ShapeDtypeStruct((D_MODEL, GLA_WIDTH), BF16),
                   jax.ShapeDtypeStruct((D_MODEL, 3 * CONV_WIDTH), BF16),
                   jax.ShapeDtypeStruct((D_MODEL, LR_PAD), BF16)],
        name="split_w_in",
    )(jnp.transpose(w_in[l]))


def _prep_layer(l, norm_ffn1, w1_ffn1, w3_ffn1, w2_ffn1, norm_mix, w_in, w_decay, b_decay, gla_norm,
                conv_w, w_out, norm_ffn2, w1_ffn2, w3_ffn2, w2_ffn2, tables):
    w_qkv, w_g, w_conv, w_lr = _split_w_in(w_in, l)
    wd = jnp.zeros((2, LR_PAD, QK_W), F32)
    wd = wd.at[0, :GLA_LOWRANK].set(w_decay[l, 0]).at[1, GLA_LOWRANK:2 * GLA_LOWRANK].set(w_decay[l, 1])
    bd = b_decay[l].reshape(2, 1, QK_W)
    return dict(
        n1=norm_ffn1[l].reshape(1, -1), w1a=w1_ffn1[l].astype(BF16), w3a=w3_ffn1[l].astype(BF16),
        w2a=w2_ffn1[l].astype(BF16),
        n_mix=norm_mix[l].reshape(1, -1), w_qkv=w_qkv, w_lr=w_lr, w_g=w_g, w_conv=w_conv,
        wd=wd.astype(BF16), bd=bd,
        gla_norm=gla_norm[l].reshape(1, -1), conv_w=conv_w[l], n2=norm_ffn2[l].reshape(1, -1), tables=tables,
        late_f32=(w_out[l], w1_ffn2[l], w3_ffn2[l], w2_ffn2[l]))


def kernel(x_prompt, x_sample, state_gla, c, c_ctx, w_ada, b_ada, norm_ffn1, w1_ffn1, w3_ffn1, w2_ffn1,
           norm_mix, w_in, w_decay, b_decay, gla_norm, conv_w, w_out, norm_ffn2, w1_ffn2, w3_ffn2,
           w2_ffn2, final_norm):
    batch, seq, _ = x_prompt.shape
    dec_batch, dec_seq, _ = x_sample.shape
    depth = w_ada.shape[0]
    assert depth == 1, "the fused final norm assumes a single layer"
    l = 0
    xp = x_prompt.reshape(batch * seq, D_MODEL)
    xs = x_sample.reshape(dec_batch * dec_seq, D_MODEL)
    mod_rows = 8
    assert 1 + dec_batch <= mod_rows
    cvec = jnp.concatenate([c_ctx[None, :], c, jnp.zeros((mod_rows - 1 - dec_batch, D_MODEL), F32)], axis=0)
    fn = final_norm.reshape(1, -1)
    lw = _prep_layer(l, norm_ffn1, w1_ffn1, w3_ffn1, w2_ffn1, norm_mix, w_in, w_decay, b_decay,
                     gla_norm, conv_w, w_out, norm_ffn2, w1_ffn2, w3_ffn2, w2_ffn2, _gla_tables())
    mods = _ada_mod(cvec, w_ada[l], b_ada[l]).reshape(mod_rows, N_MOD, D_MODEL)
    s0 = state_gla[:, l].reshape(dec_batch, 2, QK_W, GLA_DV)
    results = []
    for (x, mod_base, rows_per_mod, init, slen, period) in (
            (xp, 0, batch * seq, None, seq, seq),
            (xs, 1, dec_seq, s0, dec_seq, GRID_W)):
        if "w_out" in lw:
            x1, qkv, lr = _ffn_proj(x, mods, mod_base, rows_per_mod, lw)
        else:
            x1, qkv, lr, lw["w_out"], lw["w1b"], lw["w3b"], lw["w2b"] = _ffn_proj(
                x, mods, mod_base, rows_per_mod, lw, to_cast=lw["late_f32"])
        o_fwd, o_bwd, st = _gla(qkv, lr, init, slen, lw)
        results.append((_mix_ffn(x1, o_fwd, o_bwd, mods, mod_base, rows_per_mod, period, lw, fn), st))
    (yp, st_p), (ys, _) = results
    y_prompt = yp.reshape(batch, seq, D_MODEL)
    y_sample = ys.reshape(dec_batch, dec_seq, D_MODEL)
    new_state = st_p.reshape(batch, 1, 2, GLA_HEADS, GLA_DK, GLA_DV)
    return (y_prompt, y_sample, new_state)
```

```python
import functools
import math

import numpy as np
import jax
import jax.numpy as jnp
from jax import lax
from jax.experimental import pallas as pl
from jax.experimental.pallas import tpu as pltpu

F32 = jnp.float32
BF16 = jnp.bfloat16

D_MODEL = 1024
D_FF = 2816
N_MOD = 9
EPS = 1e-6
GLA_HEADS = 4
GLA_DK = 64
GLA_DV = 128
QK_W = GLA_HEADS * GLA_DK
GLA_WIDTH = GLA_HEADS * GLA_DV
CONV_WIDTH = 512
GLA_LOWRANK = 16
GLA_TAU = 16.0
GRID_W = 64
SPLITS = [QK_W, QK_W, GLA_WIDTH, GLA_WIDTH, GLA_LOWRANK, GLA_LOWRANK, CONV_WIDTH, CONV_WIDTH, CONV_WIDTH]

LANES = 128
ROW_GROUP = 16
CHUNK = 128
N_LEVELS = 7
N_HALF_LEVELS = 3
TILE_FFN = 512
FF_CHUNKS = ((0, D_FF),)
SCAN_TILE = 1024
FFN_LEAD = 1
LR_PAD = LANES
QKV_W = 2 * QK_W + GLA_WIDTH
VMEM_LIMIT_FFN = 56 * 1024 * 1024
VMEM_LIMIT_MIX = 48 * 1024 * 1024
LOG2E = math.log2(math.e)

_NT = (((1,), (1,)), ((), ()))


def _dot(a, b):
    return jnp.dot(a, b, preferred_element_type=F32)


def _dot_nt(a, b):
    return lax.dot_general(a, b, _NT, preferred_element_type=F32)


def _silu(x):
    return x / (1.0 + jnp.exp(-x))


def _rms(x):
    return x * lax.rsqrt(jnp.mean(x * x, axis=-1, keepdims=True) + EPS)


def _rms_mod(x, gain, mod_ref, base):
    return _rms(x) * gain * (1.0 + mod_ref[base + 1:base + 2, :]) + mod_ref[base:base + 1, :]


def _const_spec(shape):
    nd = len(shape)
    return pl.BlockSpec(shape, lambda *_: (0,) * nd, pipeline_mode=pl.Buffered(1))


def _ada_kernel(c_ref, w_ref, b_ref, o_ref):
    a = _silu(c_ref[...]).astype(BF16)
    o_ref[...] = _dot(a, w_ref[...].astype(BF16)) + b_ref[...]


def _ada_mod(cvec, w, b):
    rows = cvec.shape[0]
    n = w.shape[1]
    tn = D_MODEL
    return pl.pallas_call(
        _ada_kernel,
        grid=(n // tn,),
        in_specs=[
            pl.BlockSpec((rows, D_MODEL), lambda i: (0, 0)),
            pl.BlockSpec((D_MODEL, tn), lambda i: (0, i)),
            pl.BlockSpec((1, tn), lambda i: (0, i)),
        ],
        out_specs=pl.BlockSpec((rows, tn), lambda i: (0, i)),
        out_shape=jax.ShapeDtypeStruct((rows, n), F32),
        name="ada_mod",
    )(cvec, w, b.reshape(1, n))


def _interleave(generators, lead=0):
    active = list(generators)
    for _ in range(lead):
        next(active[0])
    while active:
        for g in list(active):
            try:
                next(g)
            except StopIteration:
                active.remove(g)


def _ffn_core(x, mod_ref, base, norm_ref, w1_ref, w3_ref, w2_ref, out):
    h = _rms_mod(x, norm_ref[...], mod_ref, base).astype(BF16)
    yield
    acc = None
    for c0, c1 in FF_CHUNKS:
        a = _dot(h, w1_ref[:, c0:c1])
        b = _dot(h, w3_ref[:, c0:c1])
        yield
        part = _dot((_silu(a) * b).astype(BF16), w2_ref[c0:c1, :])
        acc = part if acc is None else acc + part
        yield
    out.append(x + (0.5 * mod_ref[base + 2:base + 3, :]) * acc)


def _row_halves(n):
    half = n // 2
    return [slice(0, half), slice(half, n)]


def _ffn_proj_kernel(x_ref, mod_ref, n1_ref, w1_ref, w3_ref, w2_ref, n2_ref, wqkv_ref, wlr_ref, *rest):
    n_cast = (len(rest) - 3) // 2
    cast_in, (xo_ref, qkv_ref, lr_ref), cast_out = rest[:n_cast], rest[n_cast:n_cast + 3], rest[n_cast + 3:]
    for src, dst in zip(cast_in, cast_out):
        dst[...] = src[...].astype(BF16)

    def half(rows):
        out = []
        yield from _ffn_core(x_ref[rows, :], mod_ref, 0, n1_ref, w1_ref, w3_ref, w2_ref, out)
        y = out[0]
        xo_ref[rows, :] = y
        h2 = _rms_mod(y, n2_ref[...], mod_ref, 3).astype(BF16)
        yield
        qkv_ref[rows, :] = _dot(h2, wqkv_ref[...]).astype(BF16)
        yield
        lr_ref[rows, :] = _dot(h2, wlr_ref[...]).astype(BF16)

    _interleave([half(rows) for rows in _row_halves(x_ref.shape[0])], lead=FFN_LEAD)


def _mod_spec(mod_base, rows_per_mod, tile):
    per = rows_per_mod // tile
    return pl.BlockSpec((None, N_MOD, D_MODEL), lambda i: (mod_base + i // per, 0, 0))


def _ffn_weight_specs():
    return [_const_spec((1, D_MODEL)), _const_spec((D_MODEL, D_FF)), _const_spec((D_MODEL, D_FF)),
            _const_spec((D_FF, D_MODEL))]


def _ffn_proj(x, mods, mod_base, rows_per_mod, lw, to_cast=()):
    n = x.shape[0]
    tm = TILE_FFN
    steps = n // tm
    row = lambda i: (i, 0)
    cast_specs = []
    for w in to_cast:
        assert w.shape[0] % (steps * ROW_GROUP) == 0
        cast_specs.append(pl.BlockSpec((w.shape[0] // steps, w.shape[1]), row))
    return pl.pallas_call(
        _ffn_proj_kernel,
        grid=(steps,),
        in_specs=[pl.BlockSpec((tm, D_MODEL), row), _mod_spec(mod_base, rows_per_mod, tm)]
        + _ffn_weight_specs()
        + [_const_spec((1, D_MODEL)), _const_spec((D_MODEL, QKV_W)), _const_spec((D_MODEL, LR_PAD))]
        + cast_specs,
        out_specs=[
            pl.BlockSpec((tm, D_MODEL), row),
            pl.BlockSpec((tm, QKV_W), row),
            pl.BlockSpec((tm, LR_PAD), row),
        ] + cast_specs,
        out_shape=[
            jax.ShapeDtypeStruct((n, D_MODEL), F32),
            jax.ShapeDtypeStruct((n, QKV_W), BF16),
            jax.ShapeDtypeStruct((n, LR_PAD), BF16),
        ] + [jax.ShapeDtypeStruct(w.shape, BF16) for w in to_cast],
        compiler_params=pltpu.CompilerParams(
            dimension_semantics=("arbitrary",), vmem_limit_bytes=VMEM_LIMIT_FFN),
        name="ffn1_proj",
    )(x, mods, lw["n1"], lw["w1a"], lw["w3a"], lw["w2a"], lw["n_mix"], lw["w_qkv"], lw["w_lr"], *to_cast)


def _gla_tables():
    c = CHUNK
    t = np.arange(c)[:, None]
    u = np.arange(c)[None, :]
    dsum = np.zeros((2, N_LEVELS + 1, c, c), np.float32)
    mask = np.zeros((2, N_LEVELS + 1, c, c), np.float32)
    drest = np.zeros((2, c, c), np.float32)
    for lvl in range(N_LEVELS):
        m = c >> (lvl + 1)
        mid = (t // (2 * m)) * (2 * m) + m
        mid_u = (u // (2 * m)) * (2 * m) + m
        same = (t // (2 * m)) == (u // (2 * m))
        hi_t, hi_u = t >= mid, u >= mid_u
        dsum[0, lvl] = np.where(hi_t, (u >= mid) & (u <= t), (u > t) & (u < mid))
        mask[0, lvl] = same & hi_t & ~hi_u
        dsum[1, lvl] = np.where(hi_t, (u >= mid) & (u < t), (u >= t) & (u < mid))
        mask[1, lvl] = same & ~hi_t & hi_u
    dsum[0, N_LEVELS] = u <= t
    dsum[1, N_LEVELS] = u >= t
    mask[:, N_LEVELS] = t == u
    drest[0] = u > t
    drest[1] = u < t
    dsum = dsum.reshape(2, (N_LEVELS + 1) * c, c)
    dsum2 = np.concatenate([dsum, dsum], axis=2)
    rest_t = np.transpose(drest, (0, 2, 1))
    rest1 = np.concatenate([rest_t, np.ones_like(rest_t)], axis=2)
    rest2 = np.concatenate([rest1, rest1], axis=1)
    mask2 = np.concatenate([mask, mask], axis=2)
    return (jnp.asarray(dsum2, BF16), jnp.asarray(rest2, BF16), jnp.asarray(mask2, BF16))


def _query_rows(lvl, d):
    if lvl >= N_HALF_LEVELS:
        return [(0, 2 * CHUNK)]
    m = CHUNK >> (lvl + 1)
    assert m % ROW_GROUP == 0
    lo = m if d == 0 else 0
    return [(hh * CHUNK + b0 + lo, hh * CHUNK + b0 + lo + m)
            for hh in range(2) for b0 in range(0, CHUNK, 2 * m)]


def _split_hi_lo(x):
    hi = x.astype(BF16)
    lo = (x - hi.astype(F32)).astype(BF16)
    return hi, lo


def _log2_decay(x):
    log_sigmoid = jnp.minimum(x, 0.0) - jnp.log(1.0 + jnp.exp(-jnp.abs(x)))
    return log_sigmoid * (LOG2E / GLA_TAU)


def _rows(x, ranges):
    parts = [x[r0:r1] for r0, r1 in ranges]
    return parts[0] if len(parts) == 1 else jnp.concatenate(parts, axis=0)


def _gla_chunk(d, q, k, v, kt, la_hl, lat_hl, dsum_ref, rest_ref, mask_ref, state, outs):
    e_all = jnp.exp2(_dot(dsum_ref[...], la_hl)).astype(BF16)
    yield
    r = _dot(lat_hl, rest_ref[...])
    k_rest_t = kt * jnp.exp2(r[:, :CHUNK]).astype(BF16)
    gamma = jnp.exp2(r[:, CHUNK:])
    lane = lax.broadcasted_iota(jnp.int32, (CHUNK, LANES), 1)
    zero = jnp.zeros((), BF16)
    upd = []
    for pr in range(GLA_HEADS // 2):
        ls = slice(LANES * pr, LANES * (pr + 1))
        qp, kp = q[:, ls], k[:, ls]
        qm = jnp.concatenate([jnp.where(lane < GLA_DK, qp, zero),
                              jnp.where(lane >= GLA_DK, qp, zero)], axis=0)
        diag = _dot_nt(qm, kp).astype(BF16) * mask_ref[N_LEVELS]
        groups = [diag[g0:g0 + ROW_GROUP] for g0 in range(0, 2 * CHUNK, ROW_GROUP)]
        yield
        for lvl in range(N_LEVELS):
            e = e_all[CHUNK * lvl:CHUNK * (lvl + 1), ls]
            ranges = _query_rows(lvl, d)
            lhs = _rows(qm, ranges) * _rows(jnp.concatenate([e, e], axis=0), ranges)
            part = _dot_nt(lhs, kp * e).astype(BF16)
            off = 0
            for r0, r1 in ranges:
                for g0 in range(r0, r1, ROW_GROUP):
                    groups[g0 // ROW_GROUP] += part[off:off + ROW_GROUP] * mask_ref[lvl, g0:g0 + ROW_GROUP, :]
                    off += ROW_GROUP
            yield
        scores = jnp.concatenate(groups, axis=0)
        e = e_all[CHUNK * N_LEVELS:, ls]
        q_in = qm * jnp.concatenate([e, e], axis=0)
        o_inter = _dot(q_in, state[0][LANES * pr:LANES * (pr + 1), :].astype(BF16))
        for hh in range(2):
            h = 2 * pr + hh
            vh = v[:, GLA_DV * h:GLA_DV * (h + 1)]
            rows = slice(CHUNK * hh, CHUNK * (hh + 1))
            outs.append(o_inter[rows] + _dot(scores[rows], vh))
            upd.append(_dot(k_rest_t[GLA_DK * h:GLA_DK * (h + 1), :], vh))
        yield
    state[0] = gamma * state[0] + jnp.concatenate(upd, axis=0)


def _gla_kernel(*refs, nt, seqs, has_s0):
    ins, refs = refs[:6], refs[6:]
    if has_s0:
        s0_ref, refs = refs[0], refs[1:]
    (wd_ref, bd_ref, dsum_ref, rest_ref, mask_ref,
     of_ref, ob_ref, st_ref, s_ref) = refs
    carried = nt > 1 or has_s0
    assert not (carried and seqs > 1)
    n_chunks = of_ref.shape[0] // (CHUNK * seqs)

    if carried:
        @pl.when(pl.program_id(1) == 0)
        def _():
            s_ref[...] = s0_ref[0] if has_s0 else jnp.zeros(s_ref.shape, F32)

    def scan(d):
        qk_ref, v_ref, lr_ref = ins[3 * d:3 * d + 3]
        o_ref = (of_ref, ob_ref)[d]
        lr = lr_ref[...]
        la = _log2_decay(_dot(lr, wd_ref[d]) + bd_ref[d])
        la_hi, la_lo = _split_hi_lo(la)
        lat_hi, lat_lo = _split_hi_lo(la.T)
        q_all = qk_ref[:, :QK_W] * jnp.asarray(GLA_DK ** -0.5, BF16)
        k_all = qk_ref[:, QK_W:]
        kt_all = k_all.T
        yield
        for sq in range(seqs):
            state = [s_ref[d] if carried else jnp.zeros((QK_W, GLA_DV), F32)]
            for c in (range(n_chunks) if d == 0 else reversed(range(n_chunks))):
                rows = slice(CHUNK * (sq * n_chunks + c), CHUNK * (sq * n_chunks + c + 1))
                la_hl = jnp.concatenate([la_hi[rows], la_lo[rows]], axis=0)
                lat_hl = jnp.concatenate([lat_hi[:, rows], lat_lo[:, rows]], axis=1)
                outs = []
                yield from _gla_chunk(d, q_all[rows], k_all[rows], v_ref[rows, :], kt_all[:, rows], la_hl, lat_hl,
                                      dsum_ref.at[d], rest_ref.at[d], mask_ref.at[d], state, outs)
                for h in range(GLA_HEADS):
                    o_ref[rows, GLA_DV * h:GLA_DV * (h + 1)] = outs[h]
            st_ref[sq, d] = state[0]
            if carried:
                s_ref[d] = state[0]

    _interleave([scan(0), scan(1)])


def _gla(qkv, lr, s0, seq_len, lw):
    n = qkv.shape[0]
    batch = n // seq_len
    t = SCAN_TILE
    nt, seqs = max(seq_len // t, 1), max(t // seq_len, 1)
    assert nt * t == seq_len * seqs and batch % seqs == 0
    tile = (lambda b, j: b * nt + j, lambda b, j: b * nt + nt - 1 - j)
    in_specs, args = [], []
    for d in range(2):
        in_specs += [
            pl.BlockSpec((t, 2 * QK_W), lambda b, j, d=d: (tile[d](b, j), 0)),
            pl.BlockSpec((t, GLA_WIDTH), lambda b, j, d=d: (tile[d](b, j), 1)),
            pl.BlockSpec((t, LR_PAD), lambda b, j, d=d: (tile[d](b, j), 0)),
        ]
        args += [qkv, qkv, lr]
    state_spec = pl.BlockSpec((seqs, 2, QK_W, GLA_DV), lambda b, j: (b, 0, 0, 0))
    if s0 is not None:
        in_specs.append(state_spec)
        args.append(s0)
    tables = [lw["wd"], lw["bd"]] + list(lw["tables"])
    in_specs += [_const_spec(a.shape) for a in tables]
    args += tables
    return pl.pallas_call(
        functools.partial(_gla_kernel, nt=nt, seqs=seqs, has_s0=s0 is not None),
        grid=(batch // seqs, nt),
        in_specs=in_specs,
        out_specs=[
            pl.BlockSpec((t, GLA_WIDTH), lambda b, j: (tile[0](b, j), 0)),
            pl.BlockSpec((t, GLA_WIDTH), lambda b, j: (tile[1](b, j), 0)),
            state_spec,
        ],
        out_shape=[
            jax.ShapeDtypeStruct((n, GLA_WIDTH), F32),
            jax.ShapeDtypeStruct((n, GLA_WIDTH), F32),
            jax.ShapeDtypeStruct((batch, 2, QK_W, GLA_DV), F32),
        ],
        scratch_shapes=[pltpu.VMEM((2, QK_W, GLA_DV), F32)],
        compiler_params=pltpu.CompilerParams(
            dimension_semantics=("arbitrary", "arbitrary"), vmem_limit_bytes=VMEM_LIMIT_MIX),
        name="gla",
    )(*args)


def _mix_ffn_kernel(x_ref, of_ref, ob_ref, mod_ref, nm_ref, wg_ref, wc_ref, gn_ref, cw_ref, wout_ref,
                    n2_ref, w1_ref, w3_ref, w2_ref, fn_ref, yo_ref, *, period):
    def half(rows):
        x = x_ref[rows, :]
        h2 = _rms_mod(x, nm_ref[...], mod_ref, 3).astype(BF16)
        o = of_ref[rows, :] + ob_ref[rows, :]
        yield
        g = _dot(h2, wg_ref[...])
        parts = []
        for h in range(GLA_HEADS):
            cols = slice(GLA_DV * h, GLA_DV * (h + 1))
            parts.append(_rms(o[:, cols]) * gn_ref[:, cols] * _silu(g[:, cols]))
        yield
        conv = _dot(h2, wc_ref[...])
        cb = conv[:, :CONV_WIDTH]
        u = conv[:, CONV_WIDTH:2 * CONV_WIDTH] * conv[:, 2 * CONV_WIDTH:]
        pos = lax.broadcasted_iota(jnp.int32, u.shape, 0) & (period - 1)
        u_prev = jnp.where(pos == 0, 0.0, pltpu.roll(u, 1, 0))
        u_next = jnp.where(pos == period - 1, 0.0, pltpu.roll(u, u.shape[0] - 1, 0))
        cu = cw_ref[0:1, :] * u_prev + cw_ref[1:2, :] * u + cw_ref[2:3, :] * u_next
        parts.append(cb * cu)
        mixed = jnp.concatenate(parts, axis=-1).astype(BF16)
        yield
        x2 = x + mod_ref[5:6, :] * _dot(mixed, wout_ref[...])
        out = []
        yield from _ffn_core(x2, mod_ref, 6, n2_ref, w1_ref, w3_ref, w2_ref, out)
        yo_ref[rows, :] = _rms(out[0]) * fn_ref[...]

    _interleave([half(rows) for rows in _row_halves(x_ref.shape[0])], lead=FFN_LEAD)


def _mix_ffn(x, o_fwd, o_bwd, mods, mod_base, rows_per_mod, period, lw, fn):
    n = x.shape[0]
    tm = TILE_FFN
    assert period & (period - 1) == 0 and (tm // 2) % period == 0
    row = lambda i: (i, 0)
    return pl.pallas_call(
        functools.partial(_mix_ffn_kernel, period=period),
        grid=(n // tm,),
        in_specs=[pl.BlockSpec((tm, D_MODEL), row), pl.BlockSpec((tm, GLA_WIDTH), row),
                  pl.BlockSpec((tm, GLA_WIDTH), row), _mod_spec(mod_base, rows_per_mod, tm),
                  _const_spec((1, D_MODEL)), _const_spec((D_MODEL, GLA_WIDTH)),
                  _const_spec((D_MODEL, 3 * CONV_WIDTH)), _const_spec((1, GLA_WIDTH)),
                  _const_spec((3, CONV_WIDTH)), _const_spec((D_MODEL, D_MODEL))]
        + _ffn_weight_specs() + [_const_spec((1, D_MODEL))],
        out_specs=pl.BlockSpec((tm, D_MODEL), row),
        out_shape=jax.ShapeDtypeStruct((n, D_MODEL), F32),
        compiler_params=pltpu.CompilerParams(
            dimension_semantics=("arbitrary",), vmem_limit_bytes=VMEM_LIMIT_FFN),
        name="mix_ffn2",
    )(x, o_fwd, o_bwd, mods, lw["n_mix"], lw["w_g"], lw["w_conv"], lw["gla_norm"], lw["conv_w"], lw["w_out"],
      lw["n2"], lw["w1b"], lw["w3b"], lw["w2b"], fn)


def _split_w_in_kernel(wt_ref, qkv_ref, g_ref, conv_ref, lr_ref):
    edges = [0] + list(np.cumsum(SPLITS))
    piece = lambda a, b: wt_ref[edges[a]:edges[b], :]
    qkv_ref[...] = piece(0, 3).T.astype(BF16)
    g_ref[...] = piece(3, 4).T.astype(BF16)
    conv_ref[...] = piece(6, 9).T.astype(BF16)
    lr_t = jnp.concatenate([piece(4, 6), jnp.zeros((LR_PAD - 2 * GLA_LOWRANK, wt_ref.shape[1]), F32)], axis=0)
    lr_ref[...] = lr_t.T.astype(BF16)


def _split_w_in(w_in, l):
    rows = 256
    n_cols = w_in.shape[2]
    row = lambda i: (i, 0)
    return pl.pallas_call(
        _split_w_in_kernel,
        grid=(D_MODEL // rows,),
        in_specs=[pl.BlockSpec((n_cols, rows), lambda i: (0, i))],
        out_specs=[pl.BlockSpec((rows, QKV_W), row), pl.BlockSpec((rows, GLA_WIDTH), row),
                   pl.BlockSpec((rows, 3 * CONV_WIDTH), row), pl.BlockSpec((rows, LR_PAD), row)],
        out_shape=[jax.ShapeDtypeStruct((D_MODEL, QKV_W), BF16), jax.ShapeDtypeStruct((D_MODEL, GLA_WIDTH), BF16),
                   jax.ShapeDtypeStruct((D_MODEL, 3 * CONV_WIDTH), BF16),
                   jax.ShapeDtypeStruct((D_MODEL, LR_PAD), BF16)],
        name="split_w_in",
    )(jnp.transpose(w_in[l]))


def _prep_layer(l, norm_ffn1, w1_ffn1, w3_ffn1, w2_ffn1, norm_mix, w_in, w_decay, b_decay, gla_norm,
                conv_w, w_out, norm_ffn2, w1_ffn2, w3_ffn2, w2_ffn2, tables):
    w_qkv, w_g, w_conv, w_lr = _split_w_in(w_in, l)
    wd = jnp.zeros((2, LR_PAD, QK_W), F32)
    wd = wd.at[0, :GLA_LOWRANK].set(w_decay[l, 0]).at[1, GLA_LOWRANK:2 * GLA_LOWRANK].set(w_decay[l, 1])
    bd = b_decay[l].reshape(2, 1, QK_W)
    return dict(
        n1=norm_ffn1[l].reshape(1, -1), w1a=w1_ffn1[l].astype(BF16), w3a=w3_ffn1[l].astype(BF16),
        w2a=w2_ffn1[l].astype(BF16),
        n_mix=norm_mix[l].reshape(1, -1), w_qkv=w_qkv, w_lr=w_lr, w_g=w_g, w_conv=w_conv,
        wd=wd.astype(BF16), bd=bd,
        gla_norm=gla_norm[l].reshape(1, -1), conv_w=conv_w[l], n2=norm_ffn2[l].reshape(1, -1), tables=tables,
        late_f32=(w_out[l], w1_ffn2[l], w3_ffn2[l], w2_ffn2[l]))


def kernel(x_prompt, x_sample, state_gla, c, c_ctx, w_ada, b_ada, norm_ffn1, w1_ffn1, w3_ffn1, w2_ffn1,
           norm_mix, w_in, w_decay, b_decay, gla_norm, conv_w, w_out, norm_ffn2, w1_ffn2, w3_ffn2,
           w2_ffn2, final_norm):
    batch, seq, _ = x_prompt.shape
    dec_batch, dec_seq, _ = x_sample.shape
    depth = w_ada.shape[0]
    assert depth == 1, "the fused final norm assumes a single layer"
    l = 0
    xp = x_prompt.reshape(batch * seq, D_MODEL)
    xs = x_sample.reshape(dec_batch * dec_seq, D_MODEL)
    mod_rows = 8
    assert 1 + dec_batch <= mod_rows
    cvec = jnp.concatenate([c_ctx[None, :], c, jnp.zeros((mod_rows - 1 - dec_batch, D_MODEL), F32)], axis=0)
    fn = final_norm.reshape(1, -1)
    lw = _prep_layer(l, norm_ffn1, w1_ffn1, w3_ffn1, w2_ffn1, norm_mix, w_in, w_decay, b_decay,
                     gla_norm, conv_w, w_out, norm_ffn2, w1_ffn2, w3_ffn2, w2_ffn2, _gla_tables())
    mods = _ada_mod(cvec, w_ada[l], b_ada[l]).reshape(mod_rows, N_MOD, D_MODEL)
    s0 = state_gla[:, l].reshape(dec_batch, 2, QK_W, GLA_DV)
    results = []
    for (x, mod_base, rows_per_mod, init, slen, period) in (
            (xp, 0, batch * seq, None, seq, seq),
            (xs, 1, dec_seq, s0, dec_seq, GRID_W)):
        if "w_out" in lw:
            x1, qkv, lr = _ffn_proj(x, mods, mod_base, rows_per_mod, lw)
        else:
            x1, qkv, lr, lw["w_out"], lw["w1b"], lw["w3b"], lw["w2b"] = _ffn_proj(
                x, mods, mod_base, rows_per_mod, lw, to_cast=lw["late_f32"])
        o_fwd, o_bwd, st = _gla(qkv, lr, init, slen, lw)
        results.append((_mix_ffn(x1, o_fwd, o_bwd, mods, mod_base, rows_per_mod, period, lw, fn), st))
    (yp, st_p), (ys, _) = results
    y_prompt = yp.reshape(batch, seq, D_MODEL)
    y_sample = ys.reshape(dec_batch, dec_seq, D_MODEL)
    new_state = st_p.reshape(batch, 1, 2, GLA_HEADS, GLA_DK, GLA_DV)
    return (y_prompt, y_sample, new_state)
```

```python
import functools
import math

import numpy as np
import jax
import jax.numpy as jnp
from jax import lax
from jax.experimental import pallas as pl
from jax.experimental.pallas import tpu as pltpu

F32 = jnp.float32
BF16 = jnp.bfloat16

D_MODEL = 1024
D_FF = 2816
N_MOD = 9
EPS = 1e-6
GLA_HEADS = 4
GLA_DK = 64
GLA_DV = 128
QK_W = GLA_HEADS * GLA_DK
GLA_WIDTH = GLA_HEADS * GLA_DV
CONV_WIDTH = 512
GLA_LOWRANK = 16
GLA_TAU = 16.0
GRID_W = 64
SPLITS = [QK_W, QK_W, GLA_WIDTH, GLA_WIDTH, GLA_LOWRANK, GLA_LOWRANK, CONV_WIDTH, CONV_WIDTH, CONV_WIDTH]

LANES = 128
ROW_GROUP = 16
CHUNK = 128
N_LEVELS = 7
N_HALF_LEVELS = 3
TILE_FFN = 512
FF_CHUNKS = ((0, D_FF),)
ADA_STEPS = 8
SCAN_TILE = 1024
FFN_LEAD = 1
LR_PAD = LANES
QKV_W = 2 * QK_W + GLA_WIDTH
VMEM_LIMIT_FFN = 56 * 1024 * 1024
VMEM_LIMIT_MIX = 48 * 1024 * 1024
LOG2E = math.log2(math.e)

_NT = (((1,), (1,)), ((), ()))


def _dot(a, b):
    return jnp.dot(a, b, preferred_element_type=F32)


def _dot_nt(a, b):
    return lax.dot_general(a, b, _NT, preferred_element_type=F32)


def _silu(x):
    return x / (1.0 + jnp.exp(-x))


def _rms(x):
    return x * lax.rsqrt(jnp.mean(x * x, axis=-1, keepdims=True) + EPS)


def _rms_mod(x, gain, mod_ref, base):
    return _rms(x) * gain * (1.0 + mod_ref[base + 1:base + 2, :]) + mod_ref[base:base + 1, :]


def _const_spec(shape):
    nd = len(shape)
    return pl.BlockSpec(shape, lambda *_: (0,) * nd, pipeline_mode=pl.Buffered(1))


def _ada_kernel(c_ref, w_ref, b_ref, *rest):
    n_cast = (len(rest) - 1) // 2
    cast_in, o_ref, cast_out = rest[:n_cast], rest[n_cast], rest[n_cast + 1:]
    for src, dst in zip(cast_in, cast_out):
        dst[...] = src[...].astype(BF16)
    a = _silu(c_ref[...]).astype(BF16)
    o_ref[...] = _dot(a, w_ref[...].astype(BF16)) + b_ref[...]


def _ada_mod(cvec, w, b, to_cast=()):
    rows = cvec.shape[0]
    n = w.shape[1]
    steps = ADA_STEPS
    tn = n // steps
    assert tn % LANES == 0
    col = lambda i: (0, i)
    cast_specs = []
    for m in to_cast:
        assert m.shape[0] % (steps * ROW_GROUP) == 0
        cast_specs.append(pl.BlockSpec((m.shape[0] // steps, m.shape[1]), lambda i: (i, 0)))
    return pl.pallas_call(
        _ada_kernel,
        grid=(steps,),
        in_specs=[pl.BlockSpec((rows, D_MODEL), lambda i: (0, 0)), pl.BlockSpec((D_MODEL, tn), col),
                  pl.BlockSpec((1, tn), col)] + cast_specs,
        out_specs=[pl.BlockSpec((rows, tn), col)] + cast_specs,
        out_shape=[jax.ShapeDtypeStruct((rows, n), F32)] + [jax.ShapeDtypeStruct(m.shape, BF16) for m in to_cast],
        compiler_params=pltpu.CompilerParams(
            dimension_semantics=("arbitrary",), vmem_limit_bytes=VMEM_LIMIT_MIX),
        name="ada_mod",
    )(cvec, w, b.reshape(1, n), *to_cast)


def _interleave(generators, lead=0):
    active = list(generators)
    for _ in range(lead):
        next(active[0])
    while active:
        for g in list(active):
            try:
                next(g)
            except StopIteration:
                active.remove(g)


def _ffn_core(x, mod_ref, base, norm_ref, w1_ref, w3_ref, w2_ref, out):
    h = _rms_mod(x, norm_ref[...], mod_ref, base).astype(BF16)
    yield
    acc = None
    for c0, c1 in FF_CHUNKS:
        a = _dot(h, w1_ref[:, c0:c1])
        b = _dot(h, w3_ref[:, c0:c1])
        yield
        part = _dot((_silu(a) * b).astype(BF16), w2_ref[c0:c1, :])
        acc = part if acc is None else acc + part
        yield
    out.append(x + (0.5 * mod_ref[base + 2:base + 3, :]) * acc)


def _row_halves(n):
    half = n // 2
    return [slice(0, half), slice(half, n)]


def _ffn_proj_kernel(x_ref, mod_ref, n1_ref, w1_ref, w3_ref, w2_ref, n2_ref, wqkv_ref, wlr_ref, *rest):
    n_cast = (len(rest) - 3) // 2
    cast_in, (xo_ref, qkv_ref, lr_ref), cast_out = rest[:n_cast], rest[n_cast:n_cast + 3], rest[n_cast + 3:]
    for src, dst in zip(cast_in, cast_out):
        dst[...] = src[...].astype(BF16)

    def half(rows):
        out = []
        yield from _ffn_core(x_ref[rows, :], mod_ref, 0, n1_ref, w1_ref, w3_ref, w2_ref, out)
        y = out[0]
        xo_ref[rows, :] = y
        h2 = _rms_mod(y, n2_ref[...], mod_ref, 3).astype(BF16)
        yield
        qkv_ref[rows, :] = _dot(h2, wqkv_ref[...]).astype(BF16)
        yield
        lr_ref[rows, :] = _dot(h2, wlr_ref[...]).astype(BF16)

    _interleave([half(rows) for rows in _row_halves(x_ref.shape[0])], lead=FFN_LEAD)


def _mod_spec(mod_base, rows_per_mod, tile):
    per = rows_per_mod // tile
    return pl.BlockSpec((None, N_MOD, D_MODEL), lambda i: (mod_base + i // per, 0, 0))


def _ffn_weight_specs():
    return [_const_spec((1, D_MODEL)), _const_spec((D_MODEL, D_FF)), _const_spec((D_MODEL, D_FF)),
            _const_spec((D_FF, D_MODEL))]


def _ffn_proj(x, mods, mod_base, rows_per_mod, lw, to_cast=()):
    n = x.shape[0]
    tm = TILE_FFN
    steps = n // tm
    row = lambda i: (i, 0)
    cast_specs = []
    for w in to_cast:
        assert w.shape[0] % (steps * ROW_GROUP) == 0
        cast_specs.append(pl.BlockSpec((w.shape[0] // steps, w.shape[1]), row))
    return pl.pallas_call(
        _ffn_proj_kernel,
        grid=(steps,),
        in_specs=[pl.BlockSpec((tm, D_MODEL), row), _mod_spec(mod_base, rows_per_mod, tm)]
        + _ffn_weight_specs()
        + [_const_spec((1, D_MODEL)), _const_spec((D_MODEL, QKV_W)), _const_spec((D_MODEL, LR_PAD))]
        + cast_specs,
        out_specs=[
            pl.BlockSpec((tm, D_MODEL), row),
            pl.BlockSpec((tm, QKV_W), row),
            pl.BlockSpec((tm, LR_PAD), row),
        ] + cast_specs,
        out_shape=[
            jax.ShapeDtypeStruct((n, D_MODEL), F32),
            jax.ShapeDtypeStruct((n, QKV_W), BF16),
            jax.ShapeDtypeStruct((n, LR_PAD), BF16),
        ] + [jax.ShapeDtypeStruct(w.shape, BF16) for w in to_cast],
        compiler_params=pltpu.CompilerParams(
            dimension_semantics=("arbitrary",), vmem_limit_bytes=VMEM_LIMIT_FFN),
        name="ffn1_proj",
    )(x, mods, lw["n1"], lw["w1a"], lw["w3a"], lw["w2a"], lw["n_mix"], lw["w_qkv"], lw["w_lr"], *to_cast)


def _gla_tables():
    c = CHUNK
    t = np.arange(c)[:, None]
    u = np.arange(c)[None, :]
    dsum = np.zeros((2, N_LEVELS + 1, c, c), np.float32)
    mask = np.zeros((2, N_LEVELS + 1, c, c), np.float32)
    drest = np.zeros((2, c, c), np.float32)
    for lvl in range(N_LEVELS):
        m = c >> (lvl + 1)
        mid = (t // (2 * m)) * (2 * m) + m
        mid_u = (u // (2 * m)) * (2 * m) + m
        same = (t // (2 * m)) == (u // (2 * m))
        hi_t, hi_u = t >= mid, u >= mid_u
        dsum[0, lvl] = np.where(hi_t, (u >= mid) & (u <= t), (u > t) & (u < mid))
        mask[0, lvl] = same & hi_t & ~hi_u
        dsum[1, lvl] = np.where(hi_t, (u >= mid) & (u < t), (u >= t) & (u < mid))
        mask[1, lvl] = same & ~hi_t & hi_u
    dsum[0, N_LEVELS] = u <= t
    dsum[1, N_LEVELS] = u >= t
    mask[:, N_LEVELS] = t == u
    drest[0] = u > t
    drest[1] = u < t
    dsum = dsum.reshape(2, (N_LEVELS + 1) * c, c)
    dsum2 = np.concatenate([dsum, dsum], axis=2)
    rest_t = np.transpose(drest, (0, 2, 1))
    rest1 = np.concatenate([rest_t, np.ones_like(rest_t)], axis=2)
    rest2 = np.concatenate([rest1, rest1], axis=1)
    mask2 = np.concatenate([mask, mask], axis=2)
    return (jnp.asarray(dsum2, BF16), jnp.asarray(rest2, BF16), jnp.asarray(mask2, BF16))


def _query_rows(lvl, d):
    if lvl >= N_HALF_LEVELS:
        return [(0, 2 * CHUNK)]
    m = CHUNK >> (lvl + 1)
    assert m % ROW_GROUP == 0
    lo = m if d == 0 else 0
    return [(hh * CHUNK + b0 + lo, hh * CHUNK + b0 + lo + m)
            for hh in range(2) for b0 in range(0, CHUNK, 2 * m)]


def _split_hi_lo(x):
    hi = x.astype(BF16)
    lo = (x - hi.astype(F32)).astype(BF16)
    return hi, lo


def _log2_decay(x):
    log_sigmoid = jnp.minimum(x, 0.0) - jnp.log(1.0 + jnp.exp(-jnp.abs(x)))
    return log_sigmoid * (LOG2E / GLA_TAU)


def _rows(x, ranges):
    parts = [x[r0:r1] for r0, r1 in ranges]
    return parts[0] if len(parts) == 1 else jnp.concatenate(parts, axis=0)


def _gla_chunk(d, q, k, v, kt, la_hl, lat_hl, dsum_ref, rest_ref, mask_ref, state, outs):
    e_all = jnp.exp2(_dot(dsum_ref[...], la_hl)).astype(BF16)
    yield
    r = _dot(lat_hl, rest_ref[...])
    k_rest_t = kt * jnp.exp2(r[:, :CHUNK]).astype(BF16)
    gamma = jnp.exp2(r[:, CHUNK:])
    lane = lax.broadcasted_iota(jnp.int32, (CHUNK, LANES), 1)
    zero = jnp.zeros((), BF16)
    upd = []
    for pr in range(GLA_HEADS // 2):
        ls = slice(LANES * pr, LANES * (pr + 1))
        qp, kp = q[:, ls], k[:, ls]
        qm = jnp.concatenate([jnp.where(lane < GLA_DK, qp, zero),
                              jnp.where(lane >= GLA_DK, qp, zero)], axis=0)
        diag = _dot_nt(qm, kp).astype(BF16) * mask_ref[N_LEVELS]
        groups = [diag[g0:g0 + ROW_GROUP] for g0 in range(0, 2 * CHUNK, ROW_GROUP)]
        yield
        for lvl in range(N_LEVELS):
            e = e_all[CHUNK * lvl:CHUNK * (lvl + 1), ls]
            ranges = _query_rows(lvl, d)
            lhs = _rows(qm, ranges) * _rows(jnp.concatenate([e, e], axis=0), ranges)
            part = _dot_nt(lhs, kp * e).astype(BF16)
            off = 0
            for r0, r1 in ranges:
                for g0 in range(r0, r1, ROW_GROUP):
                    groups[g0 // ROW_GROUP] += part[off:off + ROW_GROUP] * mask_ref[lvl, g0:g0 + ROW_GROUP, :]
                    off += ROW_GROUP
            yield
        scores = jnp.concatenate(groups, axis=0)
        e = e_all[CHUNK * N_LEVELS:, ls]
        q_in = qm * jnp.concatenate([e, e], axis=0)
        o_inter = _dot(q_in, state[0][LANES * pr:LANES * (pr + 1), :].astype(BF16))
        for hh in range(2):
            h = 2 * pr + hh
            vh = v[:, GLA_DV * h:GLA_DV * (h + 1)]
            rows = slice(CHUNK * hh, CHUNK * (hh + 1))
            outs.append(o_inter[rows] + _dot(scores[rows], vh))
            upd.append(_dot(k_rest_t[GLA_DK * h:GLA_DK * (h + 1), :], vh))
        yield
    state[0] = gamma * state[0] + jnp.concatenate(upd, axis=0)


def _gla_kernel(*refs, nt, seqs, has_s0):
    ins, refs = refs[:6], refs[6:]
    if has_s0:
        s0_ref, refs = refs[0], refs[1:]
    (wd_ref, bd_ref, dsum_ref, rest_ref, mask_ref,
     of_ref, ob_ref, st_ref, s_ref) = refs
    carried = nt > 1 or has_s0
    assert not (carried and seqs > 1)
    n_chunks = of_ref.shape[0] // (CHUNK * seqs)

    if carried:
        @pl.when(pl.program_id(1) == 0)
        def _():
            s_ref[...] = s0_ref[0] if has_s0 else jnp.zeros(s_ref.shape, F32)

    def scan(d):
        qk_ref, v_ref, lr_ref = ins[3 * d:3 * d + 3]
        o_ref = (of_ref, ob_ref)[d]
        lr = lr_ref[...]
        la = _log2_decay(_dot(lr, wd_ref[d]) + bd_ref[d])
        la_hi, la_lo = _split_hi_lo(la)
        lat_hi, lat_lo = _split_hi_lo(la.T)
        q_all = qk_ref[:, :QK_W] * jnp.asarray(GLA_DK ** -0.5, BF16)
        k_all = qk_ref[:, QK_W:]
        kt_all = k_all.T
        yield
        for sq in range(seqs):
            state = [s_ref[d] if carried else jnp.zeros((QK_W, GLA_DV), F32)]
            for c in (range(n_chunks) if d == 0 else reversed(range(n_chunks))):
                rows = slice(CHUNK * (sq * n_chunks + c), CHUNK * (sq * n_chunks + c + 1))
                la_hl = jnp.concatenate([la_hi[rows], la_lo[rows]], axis=0)
                lat_hl = jnp.concatenate([lat_hi[:, rows], lat_lo[:, rows]], axis=1)
                outs = []
                yield from _gla_chunk(d, q_all[rows], k_all[rows], v_ref[rows, :], kt_all[:, rows], la_hl, lat_hl,
                                      dsum_ref.at[d], rest_ref.at[d], mask_ref.at[d], state, outs)
                for h in range(GLA_HEADS):
                    o_ref[rows, GLA_DV * h:GLA_DV * (h + 1)] = outs[h]
            st_ref[sq, d] = state[0]
            if carried:
                s_ref[d] = state[0]

    _interleave([scan(0), scan(1)])


def _gla(qkv, lr, s0, seq_len, lw):
    n = qkv.shape[0]
    batch = n // seq_len
    t = SCAN_TILE
    nt, seqs = max(seq_len // t, 1), max(t // seq_len, 1)
    assert nt * t == seq_len * seqs and batch % seqs == 0
    tile = (lambda b, j: b * nt + j, lambda b, j: b * nt + nt - 1 - j)
    in_specs, args = [], []
    for d in range(2):
        in_specs += [
            pl.BlockSpec((t, 2 * QK_W), lambda b, j, d=d: (tile[d](b, j), 0)),
            pl.BlockSpec((t, GLA_WIDTH), lambda b, j, d=d: (tile[d](b, j), 1)),
            pl.BlockSpec((t, LR_PAD), lambda b, j, d=d: (tile[d](b, j), 0)),
        ]
        args += [qkv, qkv, lr]
    state_spec = pl.BlockSpec((seqs, 2, QK_W, GLA_DV), lambda b, j: (b, 0, 0, 0))
    if s0 is not None:
        in_specs.append(state_spec)
        args.append(s0)
    tables = [lw["wd"], lw["bd"]] + list(lw["tables"])
    in_specs += [_const_spec(a.shape) for a in tables]
    args += tables
    return pl.pallas_call(
        functools.partial(_gla_kernel, nt=nt, seqs=seqs, has_s0=s0 is not None),
        grid=(batch // seqs, nt),
        in_specs=in_specs,
        out_specs=[
            pl.BlockSpec((t, GLA_WIDTH), lambda b, j: (tile[0](b, j), 0)),
            pl.BlockSpec((t, GLA_WIDTH), lambda b, j: (tile[1](b, j), 0)),
            state_spec,
        ],
        out_shape=[
            jax.ShapeDtypeStruct((n, GLA_WIDTH), F32),
            jax.ShapeDtypeStruct((n, GLA_WIDTH), F32),
            jax.ShapeDtypeStruct((batch, 2, QK_W, GLA_DV), F32),
        ],
        scratch_shapes=[pltpu.VMEM((2, QK_W, GLA_DV), F32)],
        compiler_params=pltpu.CompilerParams(
            dimension_semantics=("arbitrary", "arbitrary"), vmem_limit_bytes=VMEM_LIMIT_MIX),
        name="gla",
    )(*args)


def _mix_ffn_kernel(x_ref, of_ref, ob_ref, mod_ref, nm_ref, wg_ref, wc_ref, gn_ref, cw_ref, wout_ref,
                    n2_ref, w1_ref, w3_ref, w2_ref, fn_ref, yo_ref, *, period):
    def half(rows):
        x = x_ref[rows, :]
        h2 = _rms_mod(x, nm_ref[...], mod_ref, 3).astype(BF16)
        o = of_ref[rows, :] + ob_ref[rows, :]
        yield
        g = _dot(h2, wg_ref[...])
        parts = []
        for h in range(GLA_HEADS):
            cols = slice(GLA_DV * h, GLA_DV * (h + 1))
            parts.append(_rms(o[:, cols]) * gn_ref[:, cols] * _silu(g[:, cols]))
        yield
        conv = _dot(h2, wc_ref[...])
        cb = conv[:, :CONV_WIDTH]
        u = conv[:, CONV_WIDTH:2 * CONV_WIDTH] * conv[:, 2 * CONV_WIDTH:]
        pos = lax.broadcasted_iota(jnp.int32, u.shape, 0) & (period - 1)
        u_prev = jnp.where(pos == 0, 0.0, pltpu.roll(u, 1, 0))
        u_next = jnp.where(pos == period - 1, 0.0, pltpu.roll(u, u.shape[0] - 1, 0))
        cu = cw_ref[0:1, :] * u_prev + cw_ref[1:2, :] * u + cw_ref[2:3, :] * u_next
        parts.append(cb * cu)
        mixed = jnp.concatenate(parts, axis=-1).astype(BF16)
        yield
        x2 = x + mod_ref[5:6, :] * _dot(mixed, wout_ref[...])
        out = []
        yield from _ffn_core(x2, mod_ref, 6, n2_ref, w1_ref, w3_ref, w2_ref, out)
        yo_ref[rows, :] = _rms(out[0]) * fn_ref[...]

    _interleave([half(rows) for rows in _row_halves(x_ref.shape[0])], lead=FFN_LEAD)


def _mix_ffn(x, o_fwd, o_bwd, mods, mod_base, rows_per_mod, period, lw, fn):
    n = x.shape[0]
    tm = TILE_FFN
    assert period & (period - 1) == 0 and (tm // 2) % period == 0
    row = lambda i: (i, 0)
    return pl.pallas_call(
        functools.partial(_mix_ffn_kernel, period=period),
        grid=(n // tm,),
        in_specs=[pl.BlockSpec((tm, D_MODEL), row), pl.BlockSpec((tm, GLA_WIDTH), row),
                  pl.BlockSpec((tm, GLA_WIDTH), row), _mod_spec(mod_base, rows_per_mod, tm),
                  _const_spec((1, D_MODEL)), _const_spec((D_MODEL, GLA_WIDTH)),
                  _const_spec((D_MODEL, 3 * CONV_WIDTH)), _const_spec((1, GLA_WIDTH)),
                  _const_spec((3, CONV_WIDTH)), _const_spec((D_MODEL, D_MODEL))]
        + _ffn_weight_specs() + [_const_spec((1, D_MODEL))],
        out_specs=pl.BlockSpec((tm, D_MODEL), row),
        out_shape=jax.ShapeDtypeStruct((n, D_MODEL), F32),
        compiler_params=pltpu.CompilerParams(
            dimension_semantics=("arbitrary",), vmem_limit_bytes=VMEM_LIMIT_FFN),
        name="mix_ffn2",
    )(x, o_fwd, o_bwd, mods, lw["n_mix"], lw["w_g"], lw["w_conv"], lw["gla_norm"], lw["conv_w"], lw["w_out"],
      lw["n2"], lw["w1b"], lw["w3b"], lw["w2b"], fn)


def _split_w_in_kernel(wt_ref, qkv_ref, g_ref, conv_ref, lr_ref):
    edges = [0] + list(np.cumsum(SPLITS))
    piece = lambda a, b: wt_ref[edges[a]:edges[b], :]
    qkv_ref[...] = piece(0, 3).T.astype(BF16)
    g_ref[...] = piece(3, 4).T.astype(BF16)
    conv_ref[...] = piece(6, 9).T.astype(BF16)
    lr_t = jnp.concatenate([piece(4, 6), jnp.zeros((LR_PAD - 2 * GLA_LOWRANK, wt_ref.shape[1]), F32)], axis=0)
    lr_ref[...] = lr_t.T.astype(BF16)


def _split_w_in(w_in, l):
    rows = 256
    n_cols = w_in.shape[2]
    row = lambda i: (i, 0)
    return pl.pallas_call(
        _split_w_in_kernel,
        grid=(D_MODEL // rows,),
        in_specs=[pl.BlockSpec((n_cols, rows), lambda i: (0, i))],
        out_specs=[pl.BlockSpec((rows, QKV_W), row), pl.BlockSpec((rows, GLA_WIDTH), row),
                   pl.BlockSpec((rows, 3 * CONV_WIDTH), row), pl.BlockSpec((rows, LR_PAD), row)],
        out_shape=[jax.ShapeDtypeStruct((D_MODEL, QKV_W), BF16), jax.ShapeDtypeStruct((D_MODEL, GLA_WIDTH), BF16),
                   jax.ShapeDtypeStruct((D_MODEL, 3 * CONV_WIDTH), BF16),
                   jax.ShapeDtypeStruct((D_MODEL, LR_PAD), BF16)],
        name="split_w_in",
    )(jnp.transpose(w_in[l]))


def _prep_layer(l, norm_ffn1, w1_ffn1, w3_ffn1, w2_ffn1, norm_mix, w_in, w_decay, b_decay, gla_norm,
                conv_w, w_out, norm_ffn2, w1_ffn2, w3_ffn2, w2_ffn2, tables):
    w_qkv, w_g, w_conv, w_lr = _split_w_in(w_in, l)
    wd = jnp.zeros((2, LR_PAD, QK_W), F32)
    wd = wd.at[0, :GLA_LOWRANK].set(w_decay[l, 0]).at[1, GLA_LOWRANK:2 * GLA_LOWRANK].set(w_decay[l, 1])
    bd = b_decay[l].reshape(2, 1, QK_W)
    return dict(
        n1=norm_ffn1[l].reshape(1, -1), early_f32=(w1_ffn1[l], w3_ffn1[l], w2_ffn1[l]),
        n_mix=norm_mix[l].reshape(1, -1), w_qkv=w_qkv, w_lr=w_lr, w_g=w_g, w_conv=w_conv,
        wd=wd.astype(BF16), bd=bd,
        gla_norm=gla_norm[l].reshape(1, -1), conv_w=conv_w[l], n2=norm_ffn2[l].reshape(1, -1), tables=tables,
        late_f32=(w_out[l], w1_ffn2[l], w3_ffn2[l], w2_ffn2[l]))


def kernel(x_prompt, x_sample, state_gla, c, c_ctx, w_ada, b_ada, norm_ffn1, w1_ffn1, w3_ffn1, w2_ffn1,
           norm_mix, w_in, w_decay, b_decay, gla_norm, conv_w, w_out, norm_ffn2, w1_ffn2, w3_ffn2,
           w2_ffn2, final_norm):
    batch, seq, _ = x_prompt.shape
    dec_batch, dec_seq, _ = x_sample.shape
    depth = w_ada.shape[0]
    assert depth == 1, "the fused final norm assumes a single layer"
    l = 0
    xp = x_prompt.reshape(batch * seq, D_MODEL)
    xs = x_sample.reshape(dec_batch * dec_seq, D_MODEL)
    mod_rows = 8
    assert 1 + dec_batch <= mod_rows
    cvec = jnp.concatenate([c_ctx[None, :], c, jnp.zeros((mod_rows - 1 - dec_batch, D_MODEL), F32)], axis=0)
    fn = final_norm.reshape(1, -1)
    lw = _prep_layer(l, norm_ffn1, w1_ffn1, w3_ffn1, w2_ffn1, norm_mix, w_in, w_decay, b_decay,
                     gla_norm, conv_w, w_out, norm_ffn2, w1_ffn2, w3_ffn2, w2_ffn2, _gla_tables())
    mods, lw["w1a"], lw["w3a"], lw["w2a"] = _ada_mod(cvec, w_ada[l], b_ada[l], to_cast=lw["early_f32"])
    mods = mods.reshape(mod_rows, N_MOD, D_MODEL)
    s0 = state_gla[:, l].reshape(dec_batch, 2, QK_W, GLA_DV)
    results = []
    for (x, mod_base, rows_per_mod, init, slen, period) in (
            (xp, 0, batch * seq, None, seq, seq),
            (xs, 1, dec_seq, s0, dec_seq, GRID_W)):
        if "w_out" in lw:
            x1, qkv, lr = _ffn_proj(x, mods, mod_base, rows_per_mod, lw)
        else:
            x1, qkv, lr, lw["w_out"], lw["w1b"], lw["w3b"], lw["w2b"] = _ffn_proj(
                x, mods, mod_base, rows_per_mod, lw, to_cast=lw["late_f32"])
        o_fwd, o_bwd, st = _gla(qkv, lr, init, slen, lw)
        results.append((_mix_ffn(x1, o_fwd, o_bwd, mods, mod_base, rows_per_mod, period, lw, fn), st))
    (yp, st_p), (ys, _) = results
    y_prompt = yp.reshape(batch, seq, D_MODEL)
    y_sample = ys.reshape(dec_batch, dec_seq, D_MODEL)
    new_state = st_p.reshape(batch, 1, 2, GLA_HEADS, GLA_DK, GLA_DV)
    return (y_prompt, y_sample, new_state)
```

```python
import functools
import math

import numpy as np
import jax
import jax.numpy as jnp
from jax import lax
from jax.experimental import pallas as pl
from jax.experimental.pallas import tpu as pltpu

F32 = jnp.float32
BF16 = jnp.bfloat16

D_MODEL = 1024
D_FF = 2816
N_MOD = 9
EPS = 1e-6
GLA_HEADS = 4
GLA_DK = 64
GLA_DV = 128
QK_W = GLA_HEADS * GLA_DK
GLA_WIDTH = GLA_HEADS * GLA_DV
CONV_WIDTH = 512
GLA_LOWRANK = 16
GLA_TAU = 16.0
GRID_W = 64
SPLITS = [QK_W, QK_W, GLA_WIDTH, GLA_WIDTH, GLA_LOWRANK, GLA_LOWRANK, CONV_WIDTH, CONV_WIDTH, CONV_WIDTH]

LANES = 128
ROW_GROUP = 16
CHUNK = 128
N_LEVELS = 7
N_HALF_LEVELS = 3
TILE_FFN = 512
FF_CHUNKS = ((0, D_FF),)
ADA_STEPS = 8
SCAN_TILE = 1024
FFN_LEAD = 1
LR_PAD = LANES
QKV_W = 2 * QK_W + GLA_WIDTH
VMEM_LIMIT_FFN = 56 * 1024 * 1024
VMEM_LIMIT_MIX = 48 * 1024 * 1024
LOG2E = math.log2(math.e)

_NT = (((1,), (1,)), ((), ()))


def _dot(a, b):
    return jnp.dot(a, b, preferred_element_type=F32)


def _dot_nt(a, b):
    return lax.dot_general(a, b, _NT, preferred_element_type=F32)


def _silu(x):
    return x / (1.0 + jnp.exp(-x))


def _rms(x):
    return x * lax.rsqrt(jnp.mean(x * x, axis=-1, keepdims=True) + EPS)


def _rms_mod(x, gain, mod_ref, base):
    return _rms(x) * gain * (1.0 + mod_ref[base + 1:base + 2, :]) + mod_ref[base:base + 1, :]


def _const_spec(shape):
    nd = len(shape)
    return pl.BlockSpec(shape, lambda *_: (0,) * nd, pipeline_mode=pl.Buffered(1))


def _ada_kernel(c_ref, w_ref, b_ref, *rest):
    n_cast = (len(rest) - 1) // 2
    cast_in, o_ref, cast_out = rest[:n_cast], rest[n_cast], rest[n_cast + 1:]
    for src, dst in zip(cast_in, cast_out):
        dst[...] = src[...].astype(BF16)
    a = _silu(c_ref[...]).astype(BF16)
    o_ref[...] = _dot(a, w_ref[...].astype(BF16)) + b_ref[...]


def _ada_mod(cvec, w, b, to_cast=()):
    rows = cvec.shape[0]
    n = w.shape[1]
    steps = ADA_STEPS
    tn = n // steps
    assert tn % LANES == 0
    col = lambda i: (0, i)
    cast_specs = []
    for m in to_cast:
        assert m.shape[0] % (steps * ROW_GROUP) == 0
        cast_specs.append(pl.BlockSpec((m.shape[0] // steps, m.shape[1]), lambda i: (i, 0)))
    return pl.pallas_call(
        _ada_kernel,
        grid=(steps,),
        in_specs=[pl.BlockSpec((rows, D_MODEL), lambda i: (0, 0)), pl.BlockSpec((D_MODEL, tn), col),
                  pl.BlockSpec((1, tn), col)] + cast_specs,
        out_specs=[pl.BlockSpec((rows, tn), col)] + cast_specs,
        out_shape=[jax.ShapeDtypeStruct((rows, n), F32)] + [jax.ShapeDtypeStruct(m.shape, BF16) for m in to_cast],
        compiler_params=pltpu.CompilerParams(
            dimension_semantics=("arbitrary",), vmem_limit_bytes=VMEM_LIMIT_MIX),
        name="ada_mod",
    )(cvec, w, b.reshape(1, n), *to_cast)


def _interleave(generators, lead=0):
    active = list(generators)
    for _ in range(lead):
        next(active[0])
    while active:
        for g in list(active):
            try:
                next(g)
            except StopIteration:
                active.remove(g)


def _ffn_core(x, mod_ref, base, norm_ref, w1_ref, w3_ref, w2_ref, out):
    h = _rms_mod(x, norm_ref[...], mod_ref, base).astype(BF16)
    yield
    acc = None
    for c0, c1 in FF_CHUNKS:
        a = _dot(h, w1_ref[:, c0:c1])
        b = _dot(h, w3_ref[:, c0:c1])
        yield
        part = _dot((_silu(a) * b).astype(BF16), w2_ref[c0:c1, :])
        acc = part if acc is None else acc + part
        yield
    out.append(x + (0.5 * mod_ref[base + 2:base + 3, :]) * acc)


def _row_halves(n):
    half = n // 2
    return [slice(0, half), slice(half, n)]


def _ffn_proj_kernel(x_ref, mod_ref, n1_ref, w1_ref, w3_ref, w2_ref, n2_ref, wqkv_ref, wlr_ref, *rest):
    n_cast = (len(rest) - 3) // 2
    cast_in, (xo_ref, qkv_ref, lr_ref), cast_out = rest[:n_cast], rest[n_cast:n_cast + 3], rest[n_cast + 3:]
    for src, dst in zip(cast_in, cast_out):
        dst[...] = src[...].astype(BF16)

    def half(rows):
        out = []
        yield from _ffn_core(x_ref[rows, :], mod_ref, 0, n1_ref, w1_ref, w3_ref, w2_ref, out)
        y = out[0]
        xo_ref[rows, :] = y
        h2 = _rms_mod(y, n2_ref[...], mod_ref, 3).astype(BF16)
        yield
        qkv_ref[rows, :] = _dot(h2, wqkv_ref[...]).astype(BF16)
        yield
        lr_ref[rows, :] = _dot(h2, wlr_ref[...]).astype(BF16)

    _interleave([half(rows) for rows in _row_halves(x_ref.shape[0])], lead=FFN_LEAD)


def _mod_spec(mod_base, rows_per_mod, tile):
    per = rows_per_mod // tile
    return pl.BlockSpec((None, N_MOD, D_MODEL), lambda i: (mod_base + i // per, 0, 0))


def _ffn_weight_specs():
    return [_const_spec((1, D_MODEL)), _const_spec((D_MODEL, D_FF)), _const_spec((D_MODEL, D_FF)),
            _const_spec((D_FF, D_MODEL))]


def _ffn_proj(x, mods, mod_base, rows_per_mod, lw, to_cast=()):
    n = x.shape[0]
    tm = TILE_FFN
    steps = n // tm
    row = lambda i: (i, 0)
    cast_specs = []
    for w in to_cast:
        assert w.shape[0] % (steps * ROW_GROUP) == 0
        cast_specs.append(pl.BlockSpec((w.shape[0] // steps, w.shape[1]), row))
    return pl.pallas_call(
        _ffn_proj_kernel,
        grid=(steps,),
        in_specs=[pl.BlockSpec((tm, D_MODEL), row), _mod_spec(mod_base, rows_per_mod, tm)]
        + _ffn_weight_specs()
        + [_const_spec((1, D_MODEL)), _const_spec((D_MODEL, QKV_W)), _const_spec((D_MODEL, LR_PAD))]
        + cast_specs,
        out_specs=[
            pl.BlockSpec((tm, D_MODEL), row),
            pl.BlockSpec((tm, QKV_W), row),
            pl.BlockSpec((tm, LR_PAD), row),
        ] + cast_specs,
        out_shape=[
            jax.ShapeDtypeStruct((n, D_MODEL), F32),
            jax.ShapeDtypeStruct((n, QKV_W), BF16),
            jax.ShapeDtypeStruct((n, LR_PAD), BF16),
        ] + [jax.ShapeDtypeStruct(w.shape, BF16) for w in to_cast],
        compiler_params=pltpu.CompilerParams(
            dimension_semantics=("arbitrary",), vmem_limit_bytes=VMEM_LIMIT_FFN),
        name="ffn1_proj",
    )(x, mods, lw["n1"], lw["w1a"], lw["w3a"], lw["w2a"], lw["n_mix"], lw["w_qkv"], lw["w_lr"], *to_cast)


def _gla_tables():
    c = CHUNK
    t = np.arange(c)[:, None]
    u = np.arange(c)[None, :]
    dsum = np.zeros((2, N_LEVELS + 1, c, c), np.float32)
    mask = np.zeros((2, N_LEVELS + 1, c, c), np.float32)
    drest = np.zeros((2, c, c), np.float32)
    for lvl in range(N_LEVELS):
        m = c >> (lvl + 1)
        mid = (t // (2 * m)) * (2 * m) + m
        mid_u = (u // (2 * m)) * (2 * m) + m
        same = (t // (2 * m)) == (u // (2 * m))
        hi_t, hi_u = t >= mid, u >= mid_u
        dsum[0, lvl] = np.where(hi_t, (u >= mid) & (u <= t), (u > t) & (u < mid))
        mask[0, lvl] = same & hi_t & ~hi_u
        dsum[1, lvl] = np.where(hi_t, (u >= mid) & (u < t), (u >= t) & (u < mid))
        mask[1, lvl] = same & ~hi_t & hi_u
    dsum[0, N_LEVELS] = u <= t
    dsum[1, N_LEVELS] = u >= t
    mask[:, N_LEVELS] = t == u
    drest[0] = u > t
    drest[1] = u < t
    dsum = dsum.reshape(2, (N_LEVELS + 1) * c, c)
    dsum2 = np.concatenate([dsum, dsum], axis=2)
    rest_t = np.transpose(drest, (0, 2, 1))
    rest1 = np.concatenate([rest_t, np.ones_like(rest_t)], axis=2)
    rest2 = np.concatenate([rest1, rest1], axis=1)
    mask2 = np.concatenate([mask, mask], axis=2)
    return (jnp.asarray(dsum2, BF16), jnp.asarray(rest2, BF16), jnp.asarray(mask2, BF16))


def _query_rows(lvl, d):
    if lvl >= N_HALF_LEVELS:
        return [(0, 2 * CHUNK)]
    m = CHUNK >> (lvl + 1)
    assert m % ROW_GROUP == 0
    lo = m if d == 0 else 0
    return [(hh * CHUNK + b0 + lo, hh * CHUNK + b0 + lo + m)
            for hh in range(2) for b0 in range(0, CHUNK, 2 * m)]


def _split_hi_lo(x):
    hi = x.astype(BF16)
    lo = (x - hi.astype(F32)).astype(BF16)
    return hi, lo


def _log2_decay(x):
    log_sigmoid = jnp.minimum(x, 0.0) - jnp.log(1.0 + jnp.exp(-jnp.abs(x)))
    return log_sigmoid * (LOG2E / GLA_TAU)


def _rows(x, ranges):
    parts = [x[r0:r1] for r0, r1 in ranges]
    return parts[0] if len(parts) == 1 else jnp.concatenate(parts, axis=0)


def _gla_chunk(d, q, k, v, kt, la_hl, lat_hl, dsum_ref, rest_ref, mask_ref, state, outs):
    e_all = jnp.exp2(_dot(dsum_ref[...], la_hl)).astype(BF16)
    yield
    r = _dot(lat_hl, rest_ref[...])
    k_rest_t = kt * jnp.exp2(r[:, :CHUNK]).astype(BF16)
    gamma = jnp.exp2(r[:, CHUNK:])
    lane = lax.broadcasted_iota(jnp.int32, (CHUNK, LANES), 1)
    zero = jnp.zeros((), BF16)
    upd = []

    def head_pair(pr):
        ls = slice(LANES * pr, LANES * (pr + 1))
        qp, kp = q[:, ls], k[:, ls]
        qm = jnp.concatenate([jnp.where(lane < GLA_DK, qp, zero),
                              jnp.where(lane >= GLA_DK, qp, zero)], axis=0)
        diag = _dot_nt(qm, kp).astype(BF16) * mask_ref[N_LEVELS]
        groups = [diag[g0:g0 + ROW_GROUP] for g0 in range(0, 2 * CHUNK, ROW_GROUP)]
        yield
        for lvl in range(N_LEVELS):
            e = e_all[CHUNK * lvl:CHUNK * (lvl + 1), ls]
            ranges = _query_rows(lvl, d)
            lhs = _rows(qm, ranges) * _rows(jnp.concatenate([e, e], axis=0), ranges)
            part = _dot_nt(lhs, kp * e).astype(BF16)
            off = 0
            for r0, r1 in ranges:
                for g0 in range(r0, r1, ROW_GROUP):
                    groups[g0 // ROW_GROUP] += part[off:off + ROW_GROUP] * mask_ref[lvl, g0:g0 + ROW_GROUP, :]
                    off += ROW_GROUP
            yield
        scores = jnp.concatenate(groups, axis=0)
        e = e_all[CHUNK * N_LEVELS:, ls]
        q_in = qm * jnp.concatenate([e, e], axis=0)
        o_inter = _dot(q_in, state[0][LANES * pr:LANES * (pr + 1), :].astype(BF16))
        for hh in range(2):
            h = 2 * pr + hh
            vh = v[:, GLA_DV * h:GLA_DV * (h + 1)]
            rows = slice(CHUNK * hh, CHUNK * (hh + 1))
            outs.append(o_inter[rows] + _dot(scores[rows], vh))
            upd.append(_dot(k_rest_t[GLA_DK * h:GLA_DK * (h + 1), :], vh))

    pairs = [head_pair(pr) for pr in range(GLA_HEADS // 2)]
    while pairs:
        for g in list(pairs):
            try:
                next(g)
            except StopIteration:
                pairs.remove(g)
        yield
    state[0] = gamma * state[0] + jnp.concatenate(upd, axis=0)


def _gla_kernel(*refs, nt, seqs, has_s0):
    ins, refs = refs[:6], refs[6:]
    if has_s0:
        s0_ref, refs = refs[0], refs[1:]
    (wd_ref, bd_ref, dsum_ref, rest_ref, mask_ref,
     of_ref, ob_ref, st_ref, s_ref) = refs
    carried = nt > 1 or has_s0
    assert not (carried and seqs > 1)
    n_chunks = of_ref.shape[0] // (CHUNK * seqs)

    if carried:
        @pl.when(pl.program_id(1) == 0)
        def _():
            s_ref[...] = s0_ref[0] if has_s0 else jnp.zeros(s_ref.shape, F32)

    def scan(d):
        qk_ref, v_ref, lr_ref = ins[3 * d:3 * d + 3]
        o_ref = (of_ref, ob_ref)[d]
        lr = lr_ref[...]
        la = _log2_decay(_dot(lr, wd_ref[d]) + bd_ref[d])
        la_hi, la_lo = _split_hi_lo(la)
        lat_hi, lat_lo = _split_hi_lo(la.T)
        q_all = qk_ref[:, :QK_W] * jnp.asarray(GLA_DK ** -0.5, BF16)
        k_all = qk_ref[:, QK_W:]
        kt_all = k_all.T
        yield
        for sq in range(seqs):
            state = [s_ref[d] if carried else jnp.zeros((QK_W, GLA_DV), F32)]
            for c in (range(n_chunks) if d == 0 else reversed(range(n_chunks))):
                rows = slice(CHUNK * (sq * n_chunks + c), CHUNK * (sq * n_chunks + c + 1))
                la_hl = jnp.concatenate([la_hi[rows], la_lo[rows]], axis=0)
                lat_hl = jnp.concatenate([lat_hi[:, rows], lat_lo[:, rows]], axis=1)
                outs = []
                yield from _gla_chunk(d, q_all[rows], k_all[rows], v_ref[rows, :], kt_all[:, rows], la_hl, lat_hl,
                                      dsum_ref.at[d], rest_ref.at[d], mask_ref.at[d], state, outs)
                for h in range(GLA_HEADS):
                    o_ref[rows, GLA_DV * h:GLA_DV * (h + 1)] = outs[h]
            st_ref[sq, d] = state[0]
            if carried:
                s_ref[d] = state[0]

    _interleave([scan(0), scan(1)])


def _gla(qkv, lr, s0, seq_len, lw):
    n = qkv.shape[0]
    batch = n // seq_len
    t = SCAN_TILE
    nt, seqs = max(seq_len // t, 1), max(t // seq_len, 1)
    assert nt * t == seq_len * seqs and batch % seqs == 0
    tile = (lambda b, j: b * nt + j, lambda b, j: b * nt + nt - 1 - j)
    in_specs, args = [], []
    for d in range(2):
        in_specs += [
            pl.BlockSpec((t, 2 * QK_W), lambda b, j, d=d: (tile[d](b, j), 0)),
            pl.BlockSpec((t, GLA_WIDTH), lambda b, j, d=d: (tile[d](b, j), 1)),
            pl.BlockSpec((t, LR_PAD), lambda b, j, d=d: (tile[d](b, j), 0)),
        ]
        args += [qkv, qkv, lr]
    state_spec = pl.BlockSpec((seqs, 2, QK_W, GLA_DV), lambda b, j: (b, 0, 0, 0))
    if s0 is not None:
        in_specs.append(state_spec)
        args.append(s0)
    tables = [lw["wd"], lw["bd"]] + list(lw["tables"])
    in_specs += [_const_spec(a.shape) for a in tables]
    args += tables
    return pl.pallas_call(
        functools.partial(_gla_kernel, nt=nt, seqs=seqs, has_s0=s0 is not None),
        grid=(batch // seqs, nt),
        in_specs=in_specs,
        out_specs=[
            pl.BlockSpec((t, GLA_WIDTH), lambda b, j: (tile[0](b, j), 0)),
            pl.BlockSpec((t, GLA_WIDTH), lambda b, j: (tile[1](b, j), 0)),
            state_spec,
        ],
        out_shape=[
            jax.ShapeDtypeStruct((n, GLA_WIDTH), F32),
            jax.ShapeDtypeStruct((n, GLA_WIDTH), F32),
            jax.ShapeDtypeStruct((batch, 2, QK_W, GLA_DV), F32),
        ],
        scratch_shapes=[pltpu.VMEM((2, QK_W, GLA_DV), F32)],
        compiler_params=pltpu.CompilerParams(
            dimension_semantics=("arbitrary", "arbitrary"), vmem_limit_bytes=VMEM_LIMIT_MIX),
        name="gla",
    )(*args)


def _mix_ffn_kernel(x_ref, of_ref, ob_ref, mod_ref, nm_ref, wg_ref, wc_ref, gn_ref, cw_ref, wout_ref,
                    n2_ref, w1_ref, w3_ref, w2_ref, fn_ref, yo_ref, *, period):
    def half(rows):
        x = x_ref[rows, :]
        h2 = _rms_mod(x, nm_ref[...], mod_ref, 3).astype(BF16)
        o = of_ref[rows, :] + ob_ref[rows, :]
        yield
        g = _dot(h2, wg_ref[...])
        parts = []
        for h in range(GLA_HEADS):
            cols = slice(GLA_DV * h, GLA_DV * (h + 1))
            parts.append(_rms(o[:, cols]) * gn_ref[:, cols] * _silu(g[:, cols]))
        yield
        conv = _dot(h2, wc_ref[...])
        cb = conv[:, :CONV_WIDTH]
        u = conv[:, CONV_WIDTH:2 * CONV_WIDTH] * conv[:, 2 * CONV_WIDTH:]
        pos = lax.broadcasted_iota(jnp.int32, u.shape, 0) & (period - 1)
        u_prev = jnp.where(pos == 0, 0.0, pltpu.roll(u, 1, 0))
        u_next = jnp.where(pos == period - 1, 0.0, pltpu.roll(u, u.shape[0] - 1, 0))
        cu = cw_ref[0:1, :] * u_prev + cw_ref[1:2, :] * u + cw_ref[2:3, :] * u_next
        parts.append(cb * cu)
        mixed = jnp.concatenate(parts, axis=-1).astype(BF16)
        yield
        x2 = x + mod_ref[5:6, :] * _dot(mixed, wout_ref[...])
        out = []
        yield from _ffn_core(x2, mod_ref, 6, n2_ref, w1_ref, w3_ref, w2_ref, out)
        yo_ref[rows, :] = _rms(out[0]) * fn_ref[...]

    _interleave([half(rows) for rows in _row_halves(x_ref.shape[0])], lead=FFN_LEAD)


def _mix_ffn(x, o_fwd, o_bwd, mods, mod_base, rows_per_mod, period, lw, fn):
    n = x.shape[0]
    tm = TILE_FFN
    assert period & (period - 1) == 0 and (tm // 2) % period == 0
    row = lambda i: (i, 0)
    return pl.pallas_call(
        functools.partial(_mix_ffn_kernel, period=period),
        grid=(n // tm,),
        in_specs=[pl.BlockSpec((tm, D_MODEL), row), pl.BlockSpec((tm, GLA_WIDTH), row),
                  pl.BlockSpec((tm, GLA_WIDTH), row), _mod_spec(mod_base, rows_per_mod, tm),
                  _const_spec((1, D_MODEL)), _const_spec((D_MODEL, GLA_WIDTH)),
                  _const_spec((D_MODEL, 3 * CONV_WIDTH)), _const_spec((1, GLA_WIDTH)),
                  _const_spec((3, CONV_WIDTH)), _const_spec((D_MODEL, D_MODEL))]
        + _ffn_weight_specs() + [_const_spec((1, D_MODEL))],
        out_specs=pl.BlockSpec((tm, D_MODEL), row),
        out_shape=jax.ShapeDtypeStruct((n, D_MODEL), F32),
        compiler_params=pltpu.CompilerParams(
            dimension_semantics=("arbitrary",), vmem_limit_bytes=VMEM_LIMIT_FFN),
        name="mix_ffn2",
    )(x, o_fwd, o_bwd, mods, lw["n_mix"], lw["w_g"], lw["w_conv"], lw["gla_norm"], lw["conv_w"], lw["w_out"],
      lw["n2"], lw["w1b"], lw["w3b"], lw["w2b"], fn)


def _split_w_in_kernel(wt_ref, qkv_ref, g_ref, conv_ref, lr_ref):
    edges = [0] + list(np.cumsum(SPLITS))
    piece = lambda a, b: wt_ref[edges[a]:edges[b], :]
    qkv_ref[...] = piece(0, 3).T.astype(BF16)
    g_ref[...] = piece(3, 4).T.astype(BF16)
    conv_ref[...] = piece(6, 9).T.astype(BF16)
    lr_t = jnp.concatenate([piece(4, 6), jnp.zeros((LR_PAD - 2 * GLA_LOWRANK, wt_ref.shape[1]), F32)], axis=0)
    lr_ref[...] = lr_t.T.astype(BF16)


def _split_w_in(w_in, l):
    rows = 256
    n_cols = w_in.shape[2]
    row = lambda i: (i, 0)
    return pl.pallas_call(
        _split_w_in_kernel,
        grid=(D_MODEL // rows,),
        in_specs=[pl.BlockSpec((n_cols, rows), lambda i: (0, i))],
        out_specs=[pl.BlockSpec((rows, QKV_W), row), pl.BlockSpec((rows, GLA_WIDTH), row),
                   pl.BlockSpec((rows, 3 * CONV_WIDTH), row), pl.BlockSpec((rows, LR_PAD), row)],
        out_shape=[jax.ShapeDtypeStruct((D_MODEL, QKV_W), BF16), jax.ShapeDtypeStruct((D_MODEL, GLA_WIDTH), BF16),
                   jax.ShapeDtypeStruct((D_MODEL, 3 * CONV_WIDTH), BF16),
                   jax.ShapeDtypeStruct((D_MODEL, LR_PAD), BF16)],
        name="split_w_in",
    )(jnp.transpose(w_in[l]))


def _prep_layer(l, norm_ffn1, w1_ffn1, w3_ffn1, w2_ffn1, norm_mix, w_in, w_decay, b_decay, gla_norm,
                conv_w, w_out, norm_ffn2, w1_ffn2, w3_ffn2, w2_ffn2, tables):
    w_qkv, w_g, w_conv, w_lr = _split_w_in(w_in, l)
    wd = jnp.zeros((2, LR_PAD, QK_W), F32)
    wd = wd.at[0, :GLA_LOWRANK].set(w_decay[l, 0]).at[1, GLA_LOWRANK:2 * GLA_LOWRANK].set(w_decay[l, 1])
    bd = b_decay[l].reshape(2, 1, QK_W)
    return dict(
        n1=norm_ffn1[l].reshape(1, -1), early_f32=(w1_ffn1[l], w3_ffn1[l], w2_ffn1[l]),
        n_mix=norm_mix[l].reshape(1, -1), w_qkv=w_qkv, w_lr=w_lr, w_g=w_g, w_conv=w_conv,
        wd=wd.astype(BF16), bd=bd,
        gla_norm=gla_norm[l].reshape(1, -1), conv_w=conv_w[l], n2=norm_ffn2[l].reshape(1, -1), tables=tables,
        late_f32=(w_out[l], w1_ffn2[l], w3_ffn2[l], w2_ffn2[l]))


def kernel(x_prompt, x_sample, state_gla, c, c_ctx, w_ada, b_ada, norm_ffn1, w1_ffn1, w3_ffn1, w2_ffn1,
           norm_mix, w_in, w_decay, b_decay, gla_norm, conv_w, w_out, norm_ffn2, w1_ffn2, w3_ffn2,
           w2_ffn2, final_norm):
    batch, seq, _ = x_prompt.shape
    dec_batch, dec_seq, _ = x_sample.shape
    depth = w_ada.shape[0]
    assert depth == 1, "the fused final norm assumes a single layer"
    l = 0
    xp = x_prompt.reshape(batch * seq, D_MODEL)
    xs = x_sample.reshape(dec_batch * dec_seq, D_MODEL)
    mod_rows = 8
    assert 1 + dec_batch <= mod_rows
    cvec = jnp.concatenate([c_ctx[None, :], c, jnp.zeros((mod_rows - 1 - dec_batch, D_MODEL), F32)], axis=0)
    fn = final_norm.reshape(1, -1)
    lw = _prep_layer(l, norm_ffn1, w1_ffn1, w3_ffn1, w2_ffn1, norm_mix, w_in, w_decay, b_decay,
                     gla_norm, conv_w, w_out, norm_ffn2, w1_ffn2, w3_ffn2, w2_ffn2, _gla_tables())
    mods, lw["w1a"], lw["w3a"], lw["w2a"] = _ada_mod(cvec, w_ada[l], b_ada[l], to_cast=lw["early_f32"])
    mods = mods.reshape(mod_rows, N_MOD, D_MODEL)
    s0 = state_gla[:, l].reshape(dec_batch, 2, QK_W, GLA_DV)
    results = []
    for (x, mod_base, rows_per_mod, init, slen, period) in (
            (xp, 0, batch * seq, None, seq, seq),
            (xs, 1, dec_seq, s0, dec_seq, GRID_W)):
        if "w_out" in lw:
            x1, qkv, lr = _ffn_proj(x, mods, mod_base, rows_per_mod, lw)
        else:
            x1, qkv, lr, lw["w_out"], lw["w1b"], lw["w3b"], lw["w2b"] = _ffn_proj(
                x, mods, mod_base, rows_per_mod, lw, to_cast=lw["late_f32"])
        o_fwd, o_bwd, st = _gla(qkv, lr, init, slen, lw)
        results.append((_mix_ffn(x1, o_fwd, o_bwd, mods, mod_base, rows_per_mod, period, lw, fn), st))
    (yp, st_p), (ys, _) = results
    y_prompt = yp.reshape(batch, seq, D_MODEL)
    y_sample = ys.reshape(dec_batch, dec_seq, D_MODEL)
    new_state = st_p.reshape(batch, 1, 2, GLA_HEADS, GLA_DK, GLA_DV)
    return (y_prompt, y_sample, new_state)
```

```python
import functools
import math

import numpy as np
import jax
import jax.numpy as jnp
from jax import lax
from jax.experimental import pallas as pl
from jax.experimental.pallas import tpu as pltpu

F32 = jnp.float32
BF16 = jnp.bfloat16

D_MODEL = 1024
D_FF = 2816
N_MOD = 9
EPS = 1e-6
GLA_HEADS = 4
GLA_DK = 64
GLA_DV = 128
QK_W = GLA_HEADS * GLA_DK
GLA_WIDTH = GLA_HEADS * GLA_DV
CONV_WIDTH = 512
GLA_LOWRANK = 16
GLA_TAU = 16.0
GRID_W = 64
SPLITS = [QK_W, QK_W, GLA_WIDTH, GLA_WIDTH, GLA_LOWRANK, GLA_LOWRANK, CONV_WIDTH, CONV_WIDTH, CONV_WIDTH]

LANES = 128
ROW_GROUP = 16
CHUNK = 128
N_LEVELS = 7
N_HALF_LEVELS = 3
TILE_FFN = 512
FF_CHUNKS = ((0, D_FF),)
ADA_STEPS = 8
SCAN_TILE = 1024
FFN_LEAD = 1
LR_PAD = LANES
QKV_W = 2 * QK_W + GLA_WIDTH
VMEM_LIMIT_FFN = 56 * 1024 * 1024
VMEM_LIMIT_MIX = 48 * 1024 * 1024
LOG2E = math.log2(math.e)

_NT = (((1,), (1,)), ((), ()))


def _dot(a, b):
    return jnp.dot(a, b, preferred_element_type=F32)


def _dot_nt(a, b):
    return lax.dot_general(a, b, _NT, preferred_element_type=F32)


def _silu(x):
    return x / (1.0 + jnp.exp(-x))


def _rms(x):
    return x * lax.rsqrt(jnp.mean(x * x, axis=-1, keepdims=True) + EPS)


def _rms_mod(x, gain, mod_ref, base):
    return _rms(x) * gain * (1.0 + mod_ref[base + 1:base + 2, :]) + mod_ref[base:base + 1, :]


def _const_spec(shape):
    nd = len(shape)
    return pl.BlockSpec(shape, lambda *_: (0,) * nd, pipeline_mode=pl.Buffered(1))


def _ada_kernel(c_ref, w_ref, b_ref, *rest):
    n_cast = (len(rest) - 1) // 2
    cast_in, o_ref, cast_out = rest[:n_cast], rest[n_cast], rest[n_cast + 1:]
    for src, dst in zip(cast_in, cast_out):
        dst[...] = src[...].astype(BF16)
    a = _silu(c_ref[...]).astype(BF16)
    o_ref[...] = _dot(a, w_ref[...].astype(BF16)) + b_ref[...]


def _ada_mod(cvec, w, b, to_cast=()):
    rows = cvec.shape[0]
    n = w.shape[1]
    steps = ADA_STEPS
    tn = n // steps
    assert tn % LANES == 0
    col = lambda i: (0, i)
    cast_specs = []
    for m in to_cast:
        assert m.shape[0] % (steps * ROW_GROUP) == 0
        cast_specs.append(pl.BlockSpec((m.shape[0] // steps, m.shape[1]), lambda i: (i, 0)))
    return pl.pallas_call(
        _ada_kernel,
        grid=(steps,),
        in_specs=[pl.BlockSpec((rows, D_MODEL), lambda i: (0, 0)), pl.BlockSpec((D_MODEL, tn), col),
                  pl.BlockSpec((1, tn), col)] + cast_specs,
        out_specs=[pl.BlockSpec((rows, tn), col)] + cast_specs,
        out_shape=[jax.ShapeDtypeStruct((rows, n), F32)] + [jax.ShapeDtypeStruct(m.shape, BF16) for m in to_cast],
        compiler_params=pltpu.CompilerParams(
            dimension_semantics=("arbitrary",), vmem_limit_bytes=VMEM_LIMIT_MIX),
        name="ada_mod",
    )(cvec, w, b.reshape(1, n), *to_cast)


def _interleave(generators, lead=0):
    active = list(generators)
    for _ in range(lead):
        next(active[0])
    while active:
        for g in list(active):
            try:
                next(g)
            except StopIteration:
                active.remove(g)


def _ffn_core(x, mod_ref, base, norm_ref, w1_ref, w3_ref, w2_ref, out):
    h = _rms_mod(x, norm_ref[...], mod_ref, base).astype(BF16)
    yield
    acc = None
    for c0, c1 in FF_CHUNKS:
        a = _dot(h, w1_ref[:, c0:c1])
        b = _dot(h, w3_ref[:, c0:c1])
        yield
        part = _dot((_silu(a) * b).astype(BF16), w2_ref[c0:c1, :])
        acc = part if acc is None else acc + part
        yield
    out.append(x + (0.5 * mod_ref[base + 2:base + 3, :]) * acc)


def _row_halves(n):
    half = n // 2
    return [slice(0, half), slice(half, n)]


def _ffn_proj_kernel(x_ref, mod_ref, n1_ref, w1_ref, w3_ref, w2_ref, n2_ref, wqkv_ref, wlr_ref, *rest):
    n_cast = (len(rest) - 3) // 2
    cast_in, (xo_ref, qkv_ref, lr_ref), cast_out = rest[:n_cast], rest[n_cast:n_cast + 3], rest[n_cast + 3:]
    for src, dst in zip(cast_in, cast_out):
        dst[...] = src[...].astype(BF16)

    def half(rows):
        out = []
        yield from _ffn_core(x_ref[rows, :], mod_ref, 0, n1_ref, w1_ref, w3_ref, w2_ref, out)
        y = out[0]
        xo_ref[rows, :] = y
        h2 = _rms_mod(y, n2_ref[...], mod_ref, 3).astype(BF16)
        yield
        qkv_ref[rows, :] = _dot(h2, wqkv_ref[...]).astype(BF16)
        yield
        lr_ref[rows, :] = _dot(h2, wlr_ref[...]).astype(BF16)

    _interleave([half(rows) for rows in _row_halves(x_ref.shape[0])], lead=FFN_LEAD)


def _mod_spec(mod_base, rows_per_mod, tile):
    per = rows_per_mod // tile
    return pl.BlockSpec((None, N_MOD, D_MODEL), lambda i: (mod_base + i // per, 0, 0))


def _ffn_weight_specs():
    return [_const_spec((1, D_MODEL)), _const_spec((D_MODEL, D_FF)), _const_spec((D_MODEL, D_FF)),
            _const_spec((D_FF, D_MODEL))]


def _ffn_proj(x, mods, mod_base, rows_per_mod, lw, to_cast=()):
    n = x.shape[0]
    tm = TILE_FFN
    steps = n // tm
    row = lambda i: (i, 0)
    cast_specs = []
    for w in to_cast:
        assert w.shape[0] % (steps * ROW_GROUP) == 0
        cast_specs.append(pl.BlockSpec((w.shape[0] // steps, w.shape[1]), row))
    return pl.pallas_call(
        _ffn_proj_kernel,
        grid=(steps,),
        in_specs=[pl.BlockSpec((tm, D_MODEL), row), _mod_spec(mod_base, rows_per_mod, tm)]
        + _ffn_weight_specs()
        + [_const_spec((1, D_MODEL)), _const_spec((D_MODEL, QKV_W)), _const_spec((D_MODEL, LR_PAD))]
        + cast_specs,
        out_specs=[
            pl.BlockSpec((tm, D_MODEL), row),
            pl.BlockSpec((tm, QKV_W), row),
            pl.BlockSpec((tm, LR_PAD), row),
        ] + cast_specs,
        out_shape=[
            jax.ShapeDtypeStruct((n, D_MODEL), F32),
            jax.ShapeDtypeStruct((n, QKV_W), BF16),
            jax.ShapeDtypeStruct((n, LR_PAD), BF16),
        ] + [jax.ShapeDtypeStruct(w.shape, BF16) for w in to_cast],
        compiler_params=pltpu.CompilerParams(
            dimension_semantics=("arbitrary",), vmem_limit_bytes=VMEM_LIMIT_FFN),
        name="ffn1_proj",
    )(x, mods, lw["n1"], lw["w1a"], lw["w3a"], lw["w2a"], lw["n_mix"], lw["w_qkv"], lw["w_lr"], *to_cast)


def _gla_tables():
    c = CHUNK
    t = np.arange(c)[:, None]
    u = np.arange(c)[None, :]
    dsum = np.zeros((2, N_LEVELS + 1, c, c), np.float32)
    mask = np.zeros((2, N_LEVELS + 1, c, c), np.float32)
    drest = np.zeros((2, c, c), np.float32)
    for lvl in range(N_LEVELS):
        m = c >> (lvl + 1)
        mid = (t // (2 * m)) * (2 * m) + m
        mid_u = (u // (2 * m)) * (2 * m) + m
        same = (t // (2 * m)) == (u // (2 * m))
        hi_t, hi_u = t >= mid, u >= mid_u
        dsum[0, lvl] = np.where(hi_t, (u >= mid) & (u <= t), (u > t) & (u < mid))
        mask[0, lvl] = same & hi_t & ~hi_u
        dsum[1, lvl] = np.where(hi_t, (u >= mid) & (u < t), (u >= t) & (u < mid))
        mask[1, lvl] = same & ~hi_t & hi_u
    dsum[0, N_LEVELS] = u <= t
    dsum[1, N_LEVELS] = u >= t
    mask[:, N_LEVELS] = t == u
    drest[0] = u > t
    drest[1] = u < t
    dsum = dsum.reshape(2, (N_LEVELS + 1) * c, c)
    dsum2 = np.concatenate([dsum, dsum], axis=2)
    rest_t = np.transpose(drest, (0, 2, 1))
    rest1 = np.concatenate([rest_t, np.ones_like(rest_t)], axis=2)
    rest2 = np.concatenate([rest1, rest1], axis=1)
    mask2 = np.concatenate([mask, mask], axis=2)
    return (jnp.asarray(dsum2, BF16), jnp.asarray(rest2, BF16), jnp.asarray(mask2, BF16))


def _query_rows(lvl, d):
    if lvl >= N_HALF_LEVELS:
        return [(0, 2 * CHUNK)]
    m = CHUNK >> (lvl + 1)
    assert m % ROW_GROUP == 0
    lo = m if d == 0 else 0
    return [(hh * CHUNK + b0 + lo, hh * CHUNK + b0 + lo + m)
            for hh in range(2) for b0 in range(0, CHUNK, 2 * m)]


def _split_hi_lo(x):
    hi = x.astype(BF16)
    lo = (x - hi.astype(F32)).astype(BF16)
    return hi, lo


def _log2_decay(x):
    log_sigmoid = jnp.minimum(x, 0.0) - jnp.log(1.0 + jnp.exp(-jnp.abs(x)))
    return log_sigmoid * (LOG2E / GLA_TAU)


def _rows(x, ranges):
    parts = [x[r0:r1] for r0, r1 in ranges]
    return parts[0] if len(parts) == 1 else jnp.concatenate(parts, axis=0)


def _gla_chunk(d, q, k, v, kt, la_hl, lat_hl, dsum_ref, rest_ref, mask_ref, state, outs):
    e_all = jnp.exp2(_dot(dsum_ref[...], la_hl)).astype(BF16)
    yield
    r = _dot(lat_hl, rest_ref[...])
    k_rest_t = kt * jnp.exp2(r[:, :CHUNK]).astype(BF16)
    gamma = jnp.exp2(r[:, CHUNK:])
    lane = lax.broadcasted_iota(jnp.int32, (CHUNK, LANES), 1)
    zero = jnp.zeros((), BF16)
    upd = []

    def head_pair(pr):
        ls = slice(LANES * pr, LANES * (pr + 1))
        qp, kp = q[:, ls], k[:, ls]
        qm = jnp.concatenate([jnp.where(lane < GLA_DK, qp, zero),
                              jnp.where(lane >= GLA_DK, qp, zero)], axis=0)
        diag = _dot_nt(qm, kp).astype(BF16) * mask_ref[N_LEVELS]
        groups = [diag[g0:g0 + ROW_GROUP] for g0 in range(0, 2 * CHUNK, ROW_GROUP)]
        yield
        for lvl in range(N_LEVELS):
            e = e_all[CHUNK * lvl:CHUNK * (lvl + 1), ls]
            ranges = _query_rows(lvl, d)
            lhs = _rows(qm, ranges) * _rows(jnp.concatenate([e, e], axis=0), ranges)
            part = _dot_nt(lhs, kp * e).astype(BF16)
            off = 0
            for r0, r1 in ranges:
                for g0 in range(r0, r1, ROW_GROUP):
                    groups[g0 // ROW_GROUP] += part[off:off + ROW_GROUP] * mask_ref[lvl, g0:g0 + ROW_GROUP, :]
                    off += ROW_GROUP
            yield
        scores = jnp.concatenate(groups, axis=0)
        e = e_all[CHUNK * N_LEVELS:, ls]
        q_in = qm * jnp.concatenate([e, e], axis=0)
        o_inter = _dot(q_in, state[0][LANES * pr:LANES * (pr + 1), :].astype(BF16))
        for hh in range(2):
            h = 2 * pr + hh
            vh = v[:, GLA_DV * h:GLA_DV * (h + 1)]
            rows = slice(CHUNK * hh, CHUNK * (hh + 1))
            both = _dot(jnp.concatenate([scores[rows], k_rest_t[GLA_DK * h:GLA_DK * (h + 1), :]], axis=0), vh)
            outs.append(o_inter[rows] + both[:CHUNK])
            upd.append(both[CHUNK:])

    pairs = [head_pair(pr) for pr in range(GLA_HEADS // 2)]
    while pairs:
        for g in list(pairs):
            try:
                next(g)
            except StopIteration:
                pairs.remove(g)
        yield
    state[0] = gamma * state[0] + jnp.concatenate(upd, axis=0)


def _gla_kernel(*refs, nt, seqs, has_s0):
    ins, refs = refs[:6], refs[6:]
    if has_s0:
        s0_ref, refs = refs[0], refs[1:]
    (wd_ref, bd_ref, dsum_ref, rest_ref, mask_ref,
     of_ref, ob_ref, st_ref, s_ref) = refs
    carried = nt > 1 or has_s0
    assert not (carried and seqs > 1)
    n_chunks = of_ref.shape[0] // (CHUNK * seqs)

    if carried:
        @pl.when(pl.program_id(1) == 0)
        def _():
            s_ref[...] = s0_ref[0] if has_s0 else jnp.zeros(s_ref.shape, F32)

    def scan(d):
        qk_ref, v_ref, lr_ref = ins[3 * d:3 * d + 3]
        o_ref = (of_ref, ob_ref)[d]
        lr = lr_ref[...]
        la = _log2_decay(_dot(lr, wd_ref[d]) + bd_ref[d])
        la_hi, la_lo = _split_hi_lo(la)
        lat_hi, lat_lo = _split_hi_lo(la.T)
        q_all = qk_ref[:, :QK_W] * jnp.asarray(GLA_DK ** -0.5, BF16)
        k_all = qk_ref[:, QK_W:]
        kt_all = k_all.T
        yield
        for sq in range(seqs):
            state = [s_ref[d] if carried else jnp.zeros((QK_W, GLA_DV), F32)]
            for c in (range(n_chunks) if d == 0 else reversed(range(n_chunks))):
                rows = slice(CHUNK * (sq * n_chunks + c), CHUNK * (sq * n_chunks + c + 1))
                la_hl = jnp.concatenate([la_hi[rows], la_lo[rows]], axis=0)
                lat_hl = jnp.concatenate([lat_hi[:, rows], lat_lo[:, rows]], axis=1)
                outs = []
                yield from _gla_chunk(d, q_all[rows], k_all[rows], v_ref[rows, :], kt_all[:, rows], la_hl, lat_hl,
                                      dsum_ref.at[d], rest_ref.at[d], mask_ref.at[d], state, outs)
                for h in range(GLA_HEADS):
                    o_ref[rows, GLA_DV * h:GLA_DV * (h + 1)] = outs[h]
            st_ref[sq, d] = state[0]
            if carried:
                s_ref[d] = state[0]

    _interleave([scan(0), scan(1)])


def _gla(qkv, lr, s0, seq_len, lw):
    n = qkv.shape[0]
    batch = n // seq_len
    t = SCAN_TILE
    nt, seqs = max(seq_len // t, 1), max(t // seq_len, 1)
    assert nt * t == seq_len * seqs and batch % seqs == 0
    tile = (lambda b, j: b * nt + j, lambda b, j: b * nt + nt - 1 - j)
    in_specs, args = [], []
    for d in range(2):
        in_specs += [
            pl.BlockSpec((t, 2 * QK_W), lambda b, j, d=d: (tile[d](b, j), 0)),
            pl.BlockSpec((t, GLA_WIDTH), lambda b, j, d=d: (tile[d](b, j), 1)),
            pl.BlockSpec((t, LR_PAD), lambda b, j, d=d: (tile[d](b, j), 0)),
        ]
        args += [qkv, qkv, lr]
    state_spec = pl.BlockSpec((seqs, 2, QK_W, GLA_DV), lambda b, j: (b, 0, 0, 0))
    if s0 is not None:
        in_specs.append(state_spec)
        args.append(s0)
    tables = [lw["wd"], lw["bd"]] + list(lw["tables"])
    in_specs += [_const_spec(a.shape) for a in tables]
    args += tables
    return pl.pallas_call(
        functools.partial(_gla_kernel, nt=nt, seqs=seqs, has_s0=s0 is not None),
        grid=(batch // seqs, nt),
        in_specs=in_specs,
        out_specs=[
            pl.BlockSpec((t, GLA_WIDTH), lambda b, j: (tile[0](b, j), 0)),
            pl.BlockSpec((t, GLA_WIDTH), lambda b, j: (tile[1](b, j), 0)),
            state_spec,
        ],
        out_shape=[
            jax.ShapeDtypeStruct((n, GLA_WIDTH), F32),
            jax.ShapeDtypeStruct((n, GLA_WIDTH), F32),
            jax.ShapeDtypeStruct((batch, 2, QK_W, GLA_DV), F32),
        ],
        scratch_shapes=[pltpu.VMEM((2, QK_W, GLA_DV), F32)],
        compiler_params=pltpu.CompilerParams(
            dimension_semantics=("arbitrary", "arbitrary"), vmem_limit_bytes=VMEM_LIMIT_MIX),
        name="gla",
    )(*args)


def _mix_ffn_kernel(x_ref, of_ref, ob_ref, mod_ref, nm_ref, wg_ref, wc_ref, gn_ref, cw_ref, wout_ref,
                    n2_ref, w1_ref, w3_ref, w2_ref, fn_ref, yo_ref, *, period):
    def half(rows):
        x = x_ref[rows, :]
        h2 = _rms_mod(x, nm_ref[...], mod_ref, 3).astype(BF16)
        o = of_ref[rows, :] + ob_ref[rows, :]
        yield
        g = _dot(h2, wg_ref[...])
        parts = []
        for h in range(GLA_HEADS):
            cols = slice(GLA_DV * h, GLA_DV * (h + 1))
            parts.append(_rms(o[:, cols]) * gn_ref[:, cols] * _silu(g[:, cols]))
        yield
        conv = _dot(h2, wc_ref[...])
        cb = conv[:, :CONV_WIDTH]
        u = conv[:, CONV_WIDTH:2 * CONV_WIDTH] * conv[:, 2 * CONV_WIDTH:]
        pos = lax.broadcasted_iota(jnp.int32, u.shape, 0) & (period - 1)
        u_prev = jnp.where(pos == 0, 0.0, pltpu.roll(u, 1, 0))
        u_next = jnp.where(pos == period - 1, 0.0, pltpu.roll(u, u.shape[0] - 1, 0))
        cu = cw_ref[0:1, :] * u_prev + cw_ref[1:2, :] * u + cw_ref[2:3, :] * u_next
        parts.append(cb * cu)
        mixed = jnp.concatenate(parts, axis=-1).astype(BF16)
        yield
        x2 = x + mod_ref[5:6, :] * _dot(mixed, wout_ref[...])
        out = []
        yield from _ffn_core(x2, mod_ref, 6, n2_ref, w1_ref, w3_ref, w2_ref, out)
        yo_ref[rows, :] = _rms(out[0]) * fn_ref[...]

    _interleave([half(rows) for rows in _row_halves(x_ref.shape[0])], lead=FFN_LEAD)


def _mix_ffn(x, o_fwd, o_bwd, mods, mod_base, rows_per_mod, period, lw, fn):
    n = x.shape[0]
    tm = TILE_FFN
    assert period & (period - 1) == 0 and (tm // 2) % period == 0
    row = lambda i: (i, 0)
    return pl.pallas_call(
        functools.partial(_mix_ffn_kernel, period=period),
        grid=(n // tm,),
        in_specs=[pl.BlockSpec((tm, D_MODEL), row), pl.BlockSpec((tm, GLA_WIDTH), row),
                  pl.BlockSpec((tm, GLA_WIDTH), row), _mod_spec(mod_base, rows_per_mod, tm),
                  _const_spec((1, D_MODEL)), _const_spec((D_MODEL, GLA_WIDTH)),
                  _const_spec((D_MODEL, 3 * CONV_WIDTH)), _const_spec((1, GLA_WIDTH)),
                  _const_spec((3, CONV_WIDTH)), _const_spec((D_MODEL, D_MODEL))]
        + _ffn_weight_specs() + [_const_spec((1, D_MODEL))],
        out_specs=pl.BlockSpec((tm, D_MODEL), row),
        out_shape=jax.ShapeDtypeStruct((n, D_MODEL), F32),
        compiler_params=pltpu.CompilerParams(
            dimension_semantics=("arbitrary",), vmem_limit_bytes=VMEM_LIMIT_FFN),
        name="mix_ffn2",
    )(x, o_fwd, o_bwd, mods, lw["n_mix"], lw["w_g"], lw["w_conv"], lw["gla_norm"], lw["conv_w"], lw["w_out"],
      lw["n2"], lw["w1b"], lw["w3b"], lw["w2b"], fn)


def _split_w_in_kernel(wt_ref, qkv_ref, g_ref, conv_ref, lr_ref):
    edges = [0] + list(np.cumsum(SPLITS))
    piece = lambda a, b: wt_ref[edges[a]:edges[b], :]
    qkv_ref[...] = piece(0, 3).T.astype(BF16)
    g_ref[...] = piece(3, 4).T.astype(BF16)
    conv_ref[...] = piece(6, 9).T.astype(BF16)
    lr_t = jnp.concatenate([piece(4, 6), jnp.zeros((LR_PAD - 2 * GLA_LOWRANK, wt_ref.shape[1]), F32)], axis=0)
    lr_ref[...] = lr_t.T.astype(BF16)


def _split_w_in(w_in, l):
    rows = 256
    n_cols = w_in.shape[2]
    row = lambda i: (i, 0)
    return pl.pallas_call(
        _split_w_in_kernel,
        grid=(D_MODEL // rows,),
        in_specs=[pl.BlockSpec((n_cols, rows), lambda i: (0, i))],
        out_specs=[pl.BlockSpec((rows, QKV_W), row), pl.BlockSpec((rows, GLA_WIDTH), row),
                   pl.BlockSpec((rows, 3 * CONV_WIDTH), row), pl.BlockSpec((rows, LR_PAD), row)],
        out_shape=[jax.ShapeDtypeStruct((D_MODEL, QKV_W), BF16), jax.ShapeDtypeStruct((D_MODEL, GLA_WIDTH), BF16),
                   jax.ShapeDtypeStruct((D_MODEL, 3 * CONV_WIDTH), BF16),
                   jax.ShapeDtypeStruct((D_MODEL, LR_PAD), BF16)],
        name="split_w_in",
    )(jnp.transpose(w_in[l]))


def _prep_layer(l, norm_ffn1, w1_ffn1, w3_ffn1, w2_ffn1, norm_mix, w_in, w_decay, b_decay, gla_norm,
                conv_w, w_out, norm_ffn2, w1_ffn2, w3_ffn2, w2_ffn2, tables):
    w_qkv, w_g, w_conv, w_lr = _split_w_in(w_in, l)
    wd = jnp.zeros((2, LR_PAD, QK_W), F32)
    wd = wd.at[0, :GLA_LOWRANK].set(w_decay[l, 0]).at[1, GLA_LOWRANK:2 * GLA_LOWRANK].set(w_decay[l, 1])
    bd = b_decay[l].reshape(2, 1, QK_W)
    return dict(
        n1=norm_ffn1[l].reshape(1, -1), early_f32=(w1_ffn1[l], w3_ffn1[l], w2_ffn1[l]),
        n_mix=norm_mix[l].reshape(1, -1), w_qkv=w_qkv, w_lr=w_lr, w_g=w_g, w_conv=w_conv,
        wd=wd.astype(BF16), bd=bd,
        gla_norm=gla_norm[l].reshape(1, -1), conv_w=conv_w[l], n2=norm_ffn2[l].reshape(1, -1), tables=tables,
        late_f32=(w_out[l], w1_ffn2[l], w3_ffn2[l], w2_ffn2[l]))


def kernel(x_prompt, x_sample, state_gla, c, c_ctx, w_ada, b_ada, norm_ffn1, w1_ffn1, w3_ffn1, w2_ffn1,
           norm_mix, w_in, w_decay, b_decay, gla_norm, conv_w, w_out, norm_ffn2, w1_ffn2, w3_ffn2,
           w2_ffn2, final_norm):
    batch, seq, _ = x_prompt.shape
    dec_batch, dec_seq, _ = x_sample.shape
    depth = w_ada.shape[0]
    assert depth == 1, "the fused final norm assumes a single layer"
    l = 0
    xp = x_prompt.reshape(batch * seq, D_MODEL)
    xs = x_sample.reshape(dec_batch * dec_seq, D_MODEL)
    mod_rows = 8
    assert 1 + dec_batch <= mod_rows
    cvec = jnp.concatenate([c_ctx[None, :], c, jnp.zeros((mod_rows - 1 - dec_batch, D_MODEL), F32)], axis=0)
    fn = final_norm.reshape(1, -1)
    lw = _prep_layer(l, norm_ffn1, w1_ffn1, w3_ffn1, w2_ffn1, norm_mix, w_in, w_decay, b_decay,
                     gla_norm, conv_w, w_out, norm_ffn2, w1_ffn2, w3_ffn2, w2_ffn2, _gla_tables())
    mods, lw["w1a"], lw["w3a"], lw["w2a"] = _ada_mod(cvec, w_ada[l], b_ada[l], to_cast=lw["early_f32"])
    mods = mods.reshape(mod_rows, N_MOD, D_MODEL)
    s0 = state_gla[:, l].reshape(dec_batch, 2, QK_W, GLA_DV)
    results = []
    for (x, mod_base, rows_per_mod, init, slen, period) in (
            (xp, 0, batch * seq, None, seq, seq),
            (xs, 1, dec_seq, s0, dec_seq, GRID_W)):
        if "w_out" in lw:
            x1, qkv, lr = _ffn_proj(x, mods, mod_base, rows_per_mod, lw)
        else:
            x1, qkv, lr, lw["w_out"], lw["w1b"], lw["w3b"], lw["w2b"] = _ffn_proj(
                x, mods, mod_base, rows_per_mod, lw, to_cast=lw["late_f32"])
        o_fwd, o_bwd, st = _gla(qkv, lr, init, slen, lw)
        results.append((_mix_ffn(x1, o_fwd, o_bwd, mods, mod_base, rows_per_mod, period, lw, fn), st))
    (yp, st_p), (ys, _) = results
    y_prompt = yp.reshape(batch, seq, D_MODEL)
    y_sample = ys.reshape(dec_batch, dec_seq, D_MODEL)
    new_state = st_p.reshape(batch, 1, 2, GLA_HEADS, GLA_DK, GLA_DV)
    return (y_prompt, y_sample, new_state)
```

```python
import functools
import math

import numpy as np
import jax
import jax.numpy as jnp
from jax import lax
from jax.experimental import pallas as pl
from jax.experimental.pallas import tpu as pltpu

F32 = jnp.float32
BF16 = jnp.bfloat16

D_MODEL = 1024
D_FF = 2816
N_MOD = 9
EPS = 1e-6
GLA_HEADS = 4
GLA_DK = 64
GLA_DV = 128
QK_W = GLA_HEADS * GLA_DK
GLA_WIDTH = GLA_HEADS * GLA_DV
CONV_WIDTH = 512
GLA_LOWRANK = 16
GLA_TAU = 16.0
GRID_W = 64
SPLITS = [QK_W, QK_W, GLA_WIDTH, GLA_WIDTH, GLA_LOWRANK, GLA_LOWRANK, CONV_WIDTH, CONV_WIDTH, CONV_WIDTH]

LANES = 128
ROW_GROUP = 16
CHUNK = 128
N_LEVELS = 7
N_HALF_LEVELS = 3
TILE_FFN = 512
FF_CHUNKS = ((0, D_FF),)
ADA_STEPS = 8
SCAN_TILE = 1024
FFN_LEAD = 1
LR_PAD = LANES
QKV_W = 2 * QK_W + GLA_WIDTH
VMEM_LIMIT_FFN = 56 * 1024 * 1024
VMEM_LIMIT_MIX = 48 * 1024 * 1024
LOG2E = math.log2(math.e)

_NT = (((1,), (1,)), ((), ()))


def _dot(a, b):
    return jnp.dot(a, b, preferred_element_type=F32)


def _dot_nt(a, b):
    return lax.dot_general(a, b, _NT, preferred_element_type=F32)


def _silu(x):
    return x / (1.0 + jnp.exp(-x))


def _rms(x):
    return x * lax.rsqrt(jnp.mean(x * x, axis=-1, keepdims=True) + EPS)


def _rms_mod(x, gain, mod_ref, base):
    return _rms(x) * gain * (1.0 + mod_ref[base + 1:base + 2, :]) + mod_ref[base:base + 1, :]


def _const_spec(shape):
    nd = len(shape)
    return pl.BlockSpec(shape, lambda *_: (0,) * nd, pipeline_mode=pl.Buffered(1))


def _ada_kernel(c_ref, w_ref, b_ref, *rest):
    n_cast = (len(rest) - 1) // 2
    cast_in, o_ref, cast_out = rest[:n_cast], rest[n_cast], rest[n_cast + 1:]
    for src, dst in zip(cast_in, cast_out):
        dst[...] = src[...].astype(BF16)
    a = _silu(c_ref[...]).astype(BF16)
    o_ref[...] = _dot(a, w_ref[...].astype(BF16)) + b_ref[...]


def _ada_mod(cvec, w, b, to_cast=()):
    rows = cvec.shape[0]
    n = w.shape[1]
    steps = ADA_STEPS
    tn = n // steps
    assert tn % LANES == 0
    col = lambda i: (0, i)
    cast_specs = []
    for m in to_cast:
        assert m.shape[0] % (steps * ROW_GROUP) == 0
        cast_specs.append(pl.BlockSpec((m.shape[0] // steps, m.shape[1]), lambda i: (i, 0)))
    return pl.pallas_call(
        _ada_kernel,
        grid=(steps,),
        in_specs=[pl.BlockSpec((rows, D_MODEL), lambda i: (0, 0)), pl.BlockSpec((D_MODEL, tn), col),
                  pl.BlockSpec((1, tn), col)] + cast_specs,
        out_specs=[pl.BlockSpec((rows, tn), col)] + cast_specs,
        out_shape=[jax.ShapeDtypeStruct((rows, n), F32)] + [jax.ShapeDtypeStruct(m.shape, BF16) for m in to_cast],
        compiler_params=pltpu.CompilerParams(
            dimension_semantics=("arbitrary",), vmem_limit_bytes=VMEM_LIMIT_MIX),
        name="ada_mod",
    )(cvec, w, b.reshape(1, n), *to_cast)


def _interleave(generators, lead=0):
    active = list(generators)
    for _ in range(lead):
        next(active[0])
    while active:
        for g in list(active):
            try:
                next(g)
            except StopIteration:
                active.remove(g)


def _ffn_core(x, mod_ref, base, norm_ref, w1_ref, w3_ref, w2_ref, out):
    h = _rms_mod(x, norm_ref[...], mod_ref, base).astype(BF16)
    yield
    acc = None
    for c0, c1 in FF_CHUNKS:
        a = _dot(h, w1_ref[:, c0:c1])
        b = _dot(h, w3_ref[:, c0:c1])
        yield
        part = _dot((_silu(a) * b).astype(BF16), w2_ref[c0:c1, :])
        acc = part if acc is None else acc + part
        yield
    out.append(x + (0.5 * mod_ref[base + 2:base + 3, :]) * acc)


def _row_halves(n):
    half = n // 2
    return [slice(0, half), slice(half, n)]


def _ffn_proj_kernel(x_ref, mod_ref, n1_ref, w1_ref, w3_ref, w2_ref, n2_ref, wqkv_ref, wlr_ref, *rest):
    n_cast = (len(rest) - 3) // 2
    cast_in, (xo_ref, qkv_ref, lr_ref), cast_out = rest[:n_cast], rest[n_cast:n_cast + 3], rest[n_cast + 3:]
    for src, dst in zip(cast_in, cast_out):
        dst[...] = src[...].astype(BF16)

    def half(rows):
        out = []
        yield from _ffn_core(x_ref[rows, :], mod_ref, 0, n1_ref, w1_ref, w3_ref, w2_ref, out)
        y = out[0]
        xo_ref[rows, :] = y
        h2 = _rms_mod(y, n2_ref[...], mod_ref, 3).astype(BF16)
        yield
        qkv_ref[rows, :] = _dot(h2, wqkv_ref[...]).astype(BF16)
        yield
        lr_ref[rows, :] = _dot(h2, wlr_ref[...]).astype(BF16)

    _interleave([half(rows) for rows in _row_halves(x_ref.shape[0])], lead=FFN_LEAD)


def _mod_spec(mod_base, rows_per_mod, tile):
    per = rows_per_mod // tile
    return pl.BlockSpec((None, N_MOD, D_MODEL), lambda i: (mod_base + i // per, 0, 0))


def _ffn_weight_specs():
    return [_const_spec((1, D_MODEL)), _const_spec((D_MODEL, D_FF)), _const_spec((D_MODEL, D_FF)),
            _const_spec((D_FF, D_MODEL))]


def _ffn_proj(x, mods, mod_base, rows_per_mod, lw, to_cast=()):
    n = x.shape[0]
    tm = TILE_FFN
    steps = n // tm
    row = lambda i: (i, 0)
    cast_specs = []
    for w in to_cast:
        assert w.shape[0] % (steps * ROW_GROUP) == 0
        cast_specs.append(pl.BlockSpec((w.shape[0] // steps, w.shape[1]), row))
    return pl.pallas_call(
        _ffn_proj_kernel,
        grid=(steps,),
        in_specs=[pl.BlockSpec((tm, D_MODEL), row), _mod_spec(mod_base, rows_per_mod, tm)]
        + _ffn_weight_specs()
        + [_const_spec((1, D_MODEL)), _const_spec((D_MODEL, QKV_W)), _const_spec((D_MODEL, LR_PAD))]
        + cast_specs,
        out_specs=[
            pl.BlockSpec((tm, D_MODEL), row),
            pl.BlockSpec((tm, QKV_W), row),
            pl.BlockSpec((tm, LR_PAD), row),
        ] + cast_specs,
        out_shape=[
            jax.ShapeDtypeStruct((n, D_MODEL), F32),
            jax.ShapeDtypeStruct((n, QKV_W), BF16),
            jax.ShapeDtypeStruct((n, LR_PAD), BF16),
        ] + [jax.ShapeDtypeStruct(w.shape, BF16) for w in to_cast],
        compiler_params=pltpu.CompilerParams(
            dimension_semantics=("arbitrary",), vmem_limit_bytes=VMEM_LIMIT_FFN),
        name="ffn1_proj",
    )(x, mods, lw["n1"], lw["w1a"], lw["w3a"], lw["w2a"], lw["n_mix"], lw["w_qkv"], lw["w_lr"], *to_cast)


def _gla_tables():
    c = CHUNK
    t = np.arange(c)[:, None]
    u = np.arange(c)[None, :]
    dsum = np.zeros((2, N_LEVELS + 1, c, c), np.float32)
    mask = np.zeros((2, N_LEVELS + 1, c, c), np.float32)
    drest = np.zeros((2, c, c), np.float32)
    for lvl in range(N_LEVELS):
        m = c >> (lvl + 1)
        mid = (t // (2 * m)) * (2 * m) + m
        mid_u = (u // (2 * m)) * (2 * m) + m
        same = (t // (2 * m)) == (u // (2 * m))
        hi_t, hi_u = t >= mid, u >= mid_u
        dsum[0, lvl] = np.where(hi_t, (u >= mid) & (u <= t), (u > t) & (u < mid))
        mask[0, lvl] = same & hi_t & ~hi_u
        dsum[1, lvl] = np.where(hi_t, (u >= mid) & (u < t), (u >= t) & (u < mid))
        mask[1, lvl] = same & ~hi_t & hi_u
    dsum[0, N_LEVELS] = u <= t
    dsum[1, N_LEVELS] = u >= t
    mask[:, N_LEVELS] = t == u
    drest[0] = u > t
    drest[1] = u < t
    dsum = np.concatenate([dsum, drest[:, None], np.ones_like(drest)[:, None]], axis=1)
    dsum = dsum.reshape(2, (N_LEVELS + 3) * c, c)
    dsum2 = np.concatenate([dsum, dsum], axis=2)
    mask2 = np.concatenate([mask, mask], axis=2)
    return (jnp.asarray(dsum2, BF16), jnp.asarray(mask2, BF16))


def _query_rows(lvl, d):
    if lvl >= N_HALF_LEVELS:
        return [(0, 2 * CHUNK)]
    m = CHUNK >> (lvl + 1)
    assert m % ROW_GROUP == 0
    lo = m if d == 0 else 0
    return [(hh * CHUNK + b0 + lo, hh * CHUNK + b0 + lo + m)
            for hh in range(2) for b0 in range(0, CHUNK, 2 * m)]


def _split_hi_lo(x):
    hi = x.astype(BF16)
    lo = (x - hi.astype(F32)).astype(BF16)
    return hi, lo


def _log2_decay(x):
    log_sigmoid = jnp.minimum(x, 0.0) - jnp.log(1.0 + jnp.exp(-jnp.abs(x)))
    return log_sigmoid * (LOG2E / GLA_TAU)


def _rows(x, ranges):
    parts = [x[r0:r1] for r0, r1 in ranges]
    return parts[0] if len(parts) == 1 else jnp.concatenate(parts, axis=0)


def _gla_chunk(d, q, k, v, la_hl, dsum_ref, mask_ref, state, outs):
    sums = _dot(dsum_ref[...], la_hl)
    n_fac = (N_LEVELS + 1) * CHUNK
    e_all = jnp.exp2(sums[:n_fac]).astype(BF16)
    yield
    k_rest_t = (k * jnp.exp2(sums[n_fac:n_fac + CHUNK]).astype(BF16)).T
    gamma = jnp.exp2(sums[n_fac + CHUNK:]).T
    lane = lax.broadcasted_iota(jnp.int32, (CHUNK, LANES), 1)
    zero = jnp.zeros((), BF16)
    upd = []

    def head_pair(pr):
        ls = slice(LANES * pr, LANES * (pr + 1))
        qp, kp = q[:, ls], k[:, ls]
        qm = jnp.concatenate([jnp.where(lane < GLA_DK, qp, zero),
                              jnp.where(lane >= GLA_DK, qp, zero)], axis=0)
        diag = _dot_nt(qm, kp).astype(BF16) * mask_ref[N_LEVELS]
        groups = [diag[g0:g0 + ROW_GROUP] for g0 in range(0, 2 * CHUNK, ROW_GROUP)]
        yield
        for lvl in range(N_LEVELS):
            e = e_all[CHUNK * lvl:CHUNK * (lvl + 1), ls]
            ranges = _query_rows(lvl, d)
            lhs = _rows(qm, ranges) * _rows(jnp.concatenate([e, e], axis=0), ranges)
            part = _dot_nt(lhs, kp * e).astype(BF16)
            off = 0
            for r0, r1 in ranges:
                for g0 in range(r0, r1, ROW_GROUP):
                    groups[g0 // ROW_GROUP] += part[off:off + ROW_GROUP] * mask_ref[lvl, g0:g0 + ROW_GROUP, :]
                    off += ROW_GROUP
            yield
        scores = jnp.concatenate(groups, axis=0)
        e = e_all[CHUNK * N_LEVELS:, ls]
        q_in = qm * jnp.concatenate([e, e], axis=0)
        o_inter = _dot(q_in, state[0][LANES * pr:LANES * (pr + 1), :].astype(BF16))
        for hh in range(2):
            h = 2 * pr + hh
            vh = v[:, GLA_DV * h:GLA_DV * (h + 1)]
            rows = slice(CHUNK * hh, CHUNK * (hh + 1))
            both = _dot(jnp.concatenate([scores[rows], k_rest_t[GLA_DK * h:GLA_DK * (h + 1), :]], axis=0), vh)
            outs.append(o_inter[rows] + both[:CHUNK])
            upd.append(both[CHUNK:])

    pairs = [head_pair(pr) for pr in range(GLA_HEADS // 2)]
    while pairs:
        for g in list(pairs):
            try:
                next(g)
            except StopIteration:
                pairs.remove(g)
        yield
    state[0] = gamma * state[0] + jnp.concatenate(upd, axis=0)


def _gla_kernel(*refs, nt, seqs, has_s0):
    ins, refs = refs[:6], refs[6:]
    if has_s0:
        s0_ref, refs = refs[0], refs[1:]
    (wd_ref, bd_ref, dsum_ref, mask_ref,
     of_ref, ob_ref, st_ref, s_ref) = refs
    carried = nt > 1 or has_s0
    assert not (carried and seqs > 1)
    n_chunks = of_ref.shape[0] // (CHUNK * seqs)

    if carried:
        @pl.when(pl.program_id(1) == 0)
        def _():
            s_ref[...] = s0_ref[0] if has_s0 else jnp.zeros(s_ref.shape, F32)

    def scan(d):
        qk_ref, v_ref, lr_ref = ins[3 * d:3 * d + 3]
        o_ref = (of_ref, ob_ref)[d]
        lr = lr_ref[...]
        la = _log2_decay(_dot(lr, wd_ref[d]) + bd_ref[d])
        la_hi, la_lo = _split_hi_lo(la)
        q_all = qk_ref[:, :QK_W] * jnp.asarray(GLA_DK ** -0.5, BF16)
        k_all = qk_ref[:, QK_W:]
        yield
        for sq in range(seqs):
            state = [s_ref[d] if carried else jnp.zeros((QK_W, GLA_DV), F32)]
            for c in (range(n_chunks) if d == 0 else reversed(range(n_chunks))):
                rows = slice(CHUNK * (sq * n_chunks + c), CHUNK * (sq * n_chunks + c + 1))
                la_hl = jnp.concatenate([la_hi[rows], la_lo[rows]], axis=0)
                outs = []
                yield from _gla_chunk(d, q_all[rows], k_all[rows], v_ref[rows, :], la_hl,
                                      dsum_ref.at[d], mask_ref.at[d], state, outs)
                for h in range(GLA_HEADS):
                    o_ref[rows, GLA_DV * h:GLA_DV * (h + 1)] = outs[h]
            st_ref[sq, d] = state[0]
            if carried:
                s_ref[d] = state[0]

    _interleave([scan(0), scan(1)])


def _gla(qkv, lr, s0, seq_len, lw):
    n = qkv.shape[0]
    batch = n // seq_len
    t = SCAN_TILE
    nt, seqs = max(seq_len // t, 1), max(t // seq_len, 1)
    assert nt * t == seq_len * seqs and batch % seqs == 0
    tile = (lambda b, j: b * nt + j, lambda b, j: b * nt + nt - 1 - j)
    in_specs, args = [], []
    for d in range(2):
        in_specs += [
            pl.BlockSpec((t, 2 * QK_W), lambda b, j, d=d: (tile[d](b, j), 0)),
            pl.BlockSpec((t, GLA_WIDTH), lambda b, j, d=d: (tile[d](b, j), 1)),
            pl.BlockSpec((t, LR_PAD), lambda b, j, d=d: (tile[d](b, j), 0)),
        ]
        args += [qkv, qkv, lr]
    state_spec = pl.BlockSpec((seqs, 2, QK_W, GLA_DV), lambda b, j: (b, 0, 0, 0))
    if s0 is not None:
        in_specs.append(state_spec)
        args.append(s0)
    tables = [lw["wd"], lw["bd"]] + list(lw["tables"])
    in_specs += [_const_spec(a.shape) for a in tables]
    args += tables
    return pl.pallas_call(
        functools.partial(_gla_kernel, nt=nt, seqs=seqs, has_s0=s0 is not None),
        grid=(batch // seqs, nt),
        in_specs=in_specs,
        out_specs=[
            pl.BlockSpec((t, GLA_WIDTH), lambda b, j: (tile[0](b, j), 0)),
            pl.BlockSpec((t, GLA_WIDTH), lambda b, j: (tile[1](b, j), 0)),
            state_spec,
        ],
        out_shape=[
            jax.ShapeDtypeStruct((n, GLA_WIDTH), F32),
            jax.ShapeDtypeStruct((n, GLA_WIDTH), F32),
            jax.ShapeDtypeStruct((batch, 2, QK_W, GLA_DV), F32),
        ],
        scratch_shapes=[pltpu.VMEM((2, QK_W, GLA_DV), F32)],
        compiler_params=pltpu.CompilerParams(
            dimension_semantics=("arbitrary", "arbitrary"), vmem_limit_bytes=VMEM_LIMIT_MIX),
        name="gla",
    )(*args)


def _mix_ffn_kernel(x_ref, of_ref, ob_ref, mod_ref, nm_ref, wg_ref, wc_ref, gn_ref, cw_ref, wout_ref,
                    n2_ref, w1_ref, w3_ref, w2_ref, fn_ref, yo_ref, *, period):
    def half(rows):
        x = x_ref[rows, :]
        h2 = _rms_mod(x, nm_ref[...], mod_ref, 3).astype(BF16)
        o = of_ref[rows, :] + ob_ref[rows, :]
        yield
        g = _dot(h2, wg_ref[...])
        parts = []
        for h in range(GLA_HEADS):
            cols = slice(GLA_DV * h, GLA_DV * (h + 1))
            parts.append(_rms(o[:, cols]) * gn_ref[:, cols] * _silu(g[:, cols]))
        yield
        conv = _dot(h2, wc_ref[...])
        cb = conv[:, :CONV_WIDTH]
        u = conv[:, CONV_WIDTH:2 * CONV_WIDTH] * conv[:, 2 * CONV_WIDTH:]
        pos = lax.broadcasted_iota(jnp.int32, u.shape, 0) & (period - 1)
        u_prev = jnp.where(pos == 0, 0.0, pltpu.roll(u, 1, 0))
        u_next = jnp.where(pos == period - 1, 0.0, pltpu.roll(u, u.shape[0] - 1, 0))
        cu = cw_ref[0:1, :] * u_prev + cw_ref[1:2, :] * u + cw_ref[2:3, :] * u_next
        parts.append(cb * cu)
        mixed = jnp.concatenate(parts, axis=-1).astype(BF16)
        yield
        x2 = x + mod_ref[5:6, :] * _dot(mixed, wout_ref[...])
        out = []
        yield from _ffn_core(x2, mod_ref, 6, n2_ref, w1_ref, w3_ref, w2_ref, out)
        yo_ref[rows, :] = _rms(out[0]) * fn_ref[...]

    _interleave([half(rows) for rows in _row_halves(x_ref.shape[0])], lead=FFN_LEAD)


def _mix_ffn(x, o_fwd, o_bwd, mods, mod_base, rows_per_mod, period, lw, fn):
    n = x.shape[0]
    tm = TILE_FFN
    assert period & (period - 1) == 0 and (tm // 2) % period == 0
    row = lambda i: (i, 0)
    return pl.pallas_call(
        functools.partial(_mix_ffn_kernel, period=period),
        grid=(n // tm,),
        in_specs=[pl.BlockSpec((tm, D_MODEL), row), pl.BlockSpec((tm, GLA_WIDTH), row),
                  pl.BlockSpec((tm, GLA_WIDTH), row), _mod_spec(mod_base, rows_per_mod, tm),
                  _const_spec((1, D_MODEL)), _const_spec((D_MODEL, GLA_WIDTH)),
                  _const_spec((D_MODEL, 3 * CONV_WIDTH)), _const_spec((1, GLA_WIDTH)),
                  _const_spec((3, CONV_WIDTH)), _const_spec((D_MODEL, D_MODEL))]
        + _ffn_weight_specs() + [_const_spec((1, D_MODEL))],
        out_specs=pl.BlockSpec((tm, D_MODEL), row),
        out_shape=jax.ShapeDtypeStruct((n, D_MODEL), F32),
        compiler_params=pltpu.CompilerParams(
            dimension_semantics=("arbitrary",), vmem_limit_bytes=VMEM_LIMIT_FFN),
        name="mix_ffn2",
    )(x, o_fwd, o_bwd, mods, lw["n_mix"], lw["w_g"], lw["w_conv"], lw["gla_norm"], lw["conv_w"], lw["w_out"],
      lw["n2"], lw["w1b"], lw["w3b"], lw["w2b"], fn)


def _split_w_in_kernel(wt_ref, qkv_ref, g_ref, conv_ref, lr_ref):
    edges = [0] + list(np.cumsum(SPLITS))
    piece = lambda a, b: wt_ref[edges[a]:edges[b], :]
    qkv_ref[...] = piece(0, 3).T.astype(BF16)
    g_ref[...] = piece(3, 4).T.astype(BF16)
    conv_ref[...] = piece(6, 9).T.astype(BF16)
    lr_t = jnp.concatenate([piece(4, 6), jnp.zeros((LR_PAD - 2 * GLA_LOWRANK, wt_ref.shape[1]), F32)], axis=0)
    lr_ref[...] = lr_t.T.astype(BF16)


def _split_w_in(w_in, l):
    rows = 256
    n_cols = w_in.shape[2]
    row = lambda i: (i, 0)
    return pl.pallas_call(
        _split_w_in_kernel,
        grid=(D_MODEL // rows,),
        in_specs=[pl.BlockSpec((n_cols, rows), lambda i: (0, i))],
        out_specs=[pl.BlockSpec((rows, QKV_W), row), pl.BlockSpec((rows, GLA_WIDTH), row),
                   pl.BlockSpec((rows, 3 * CONV_WIDTH), row), pl.BlockSpec((rows, LR_PAD), row)],
        out_shape=[jax.ShapeDtypeStruct((D_MODEL, QKV_W), BF16), jax.ShapeDtypeStruct((D_MODEL, GLA_WIDTH), BF16),
                   jax.ShapeDtypeStruct((D_MODEL, 3 * CONV_WIDTH), BF16),
                   jax.ShapeDtypeStruct((D_MODEL, LR_PAD), BF16)],
        name="split_w_in",
    )(jnp.transpose(w_in[l]))


def _prep_layer(l, norm_ffn1, w1_ffn1, w3_ffn1, w2_ffn1, norm_mix, w_in, w_decay, b_decay, gla_norm,
                conv_w, w_out, norm_ffn2, w1_ffn2, w3_ffn2, w2_ffn2, tables):
    w_qkv, w_g, w_conv, w_lr = _split_w_in(w_in, l)
    wd = jnp.zeros((2, LR_PAD, QK_W), F32)
    wd = wd.at[0, :GLA_LOWRANK].set(w_decay[l, 0]).at[1, GLA_LOWRANK:2 * GLA_LOWRANK].set(w_decay[l, 1])
    bd = b_decay[l].reshape(2, 1, QK_W)
    return dict(
        n1=norm_ffn1[l].reshape(1, -1), early_f32=(w1_ffn1[l], w3_ffn1[l], w2_ffn1[l]),
        n_mix=norm_mix[l].reshape(1, -1), w_qkv=w_qkv, w_lr=w_lr, w_g=w_g, w_conv=w_conv,
        wd=wd.astype(BF16), bd=bd,
        gla_norm=gla_norm[l].reshape(1, -1), conv_w=conv_w[l], n2=norm_ffn2[l].reshape(1, -1), tables=tables,
        late_f32=(w_out[l], w1_ffn2[l], w3_ffn2[l], w2_ffn2[l]))


def kernel(x_prompt, x_sample, state_gla, c, c_ctx, w_ada, b_ada, norm_ffn1, w1_ffn1, w3_ffn1, w2_ffn1,
           norm_mix, w_in, w_decay, b_decay, gla_norm, conv_w, w_out, norm_ffn2, w1_ffn2, w3_ffn2,
           w2_ffn2, final_norm):
    batch, seq, _ = x_prompt.shape
    dec_batch, dec_seq, _ = x_sample.shape
    depth = w_ada.shape[0]
    assert depth == 1, "the fused final norm assumes a single layer"
    l = 0
    xp = x_prompt.reshape(batch * seq, D_MODEL)
    xs = x_sample.reshape(dec_batch * dec_seq, D_MODEL)
    mod_rows = 8
    assert 1 + dec_batch <= mod_rows
    cvec = jnp.concatenate([c_ctx[None, :], c, jnp.zeros((mod_rows - 1 - dec_batch, D_MODEL), F32)], axis=0)
    fn = final_norm.reshape(1, -1)
    lw = _prep_layer(l, norm_ffn1, w1_ffn1, w3_ffn1, w2_ffn1, norm_mix, w_in, w_decay, b_decay,
                     gla_norm, conv_w, w_out, norm_ffn2, w1_ffn2, w3_ffn2, w2_ffn2, _gla_tables())
    mods, lw["w1a"], lw["w3a"], lw["w2a"] = _ada_mod(cvec, w_ada[l], b_ada[l], to_cast=lw["early_f32"])
    mods = mods.reshape(mod_rows, N_MOD, D_MODEL)
    s0 = state_gla[:, l].reshape(dec_batch, 2, QK_W, GLA_DV)
    results = []
    for (x, mod_base, rows_per_mod, init, slen, period) in (
            (xp, 0, batch * seq, None, seq, seq),
            (xs, 1, dec_seq, s0, dec_seq, GRID_W)):
        if "w_out" in lw:
            x1, qkv, lr = _ffn_proj(x, mods, mod_base, rows_per_mod, lw)
        else:
            x1, qkv, lr, lw["w_out"], lw["w1b"], lw["w3b"], lw["w2b"] = _ffn_proj(
                x, mods, mod_base, rows_per_mod, lw, to_cast=lw["late_f32"])
        o_fwd, o_bwd, st = _gla(qkv, lr, init, slen, lw)
        results.append((_mix_ffn(x1, o_fwd, o_bwd, mods, mod_base, rows_per_mod, period, lw, fn), st))
    (yp, st_p), (ys, _) = results
    y_prompt = yp.reshape(batch, seq, D_MODEL)
    y_sample = ys.reshape(dec_batch, dec_seq, D_MODEL)
    new_state = st_p.reshape(batch, 1, 2, GLA_HEADS, GLA_DK, GLA_DV)
    return (y_prompt, y_sample, new_state)
```
